```python
import math
import jax, jax.numpy as jnp
from jax import lax
import numpy as np

D_MODEL = 4096
BATCH = 4
SEQ = 2048
DEPTH = 4
DEC_BATCH = 8
DEC_SEQ = 8
PAST_LEN = 8192
PAGE_SIZE = 128

A_HEADS = 8
A_DK = 128
A_DV = 128
A_WIDTH = A_HEADS * A_DV
A_CHUNK = 64
B_HEADS = 4
B_DK = 128
B_DV = 256
B_WIDTH = B_HEADS * B_DV
B_CHUNK = 64
GATE_SOFTCAP = 15.0
C_HEADS = 16
C_DH = 128
C_WIDTH = C_HEADS * C_DH
C_SCALE = C_DH ** -0.5
Q_BLOCK = 128
D_FF = 11008
EPS = 1e-6
NEG_BIG = -1e30
TINY = 1e-30

IN_SPLIT = (A_HEADS * A_DK, A_HEADS * A_DK, A_WIDTH, A_WIDTH,
            B_HEADS * B_DK, B_HEADS * B_DK, B_WIDTH, B_WIDTH, B_HEADS, B_HEADS,
            C_WIDTH, C_WIDTH, C_WIDTH, C_HEADS,
            D_MODEL, D_MODEL, D_MODEL)
IN_COLS = sum(IN_SPLIT)
IN_OFFSETS = tuple(int(o) for o in np.cumsum(IN_SPLIT)[:-1])

kernel_name = "hybrid_hgrn2_mlstm_fox_macaron_step"


def _rmsnorm(x, g):
    xf = x.astype(jnp.float32)
    y = xf * lax.rsqrt(jnp.mean(xf * xf, axis=-1, keepdims=True) + EPS)
    return (y * g.astype(jnp.float32)).astype(x.dtype)


def _swiglu(h, w1, w3, w2):
    return (jax.nn.silu(h @ w1) * (h @ w3)) @ w2


def _softcap(z):
    return GATE_SOFTCAP * jnp.tanh(z / GATE_SOFTCAP)


def _to_chunks(z, c):
    b, t = z.shape[:2]
    return z.reshape((b, t // c, c) + z.shape[2:]).swapaxes(0, 1)


def _from_chunks(z):
    n, b, c = z.shape[:3]
    return z.swapaxes(0, 1).reshape((b, n * c) + z.shape[3:])


def _hgrn2_scan(q, k, v, logf, s0):
    t = q.shape[1]
    c = math.gcd(t, A_CHUNK)
    causal = jnp.tril(jnp.ones((c, c), dtype=bool))[None, :, :, None, None]

    def step(s, inp):
        qc, kc, vc, lf = inp
        g = jnp.cumsum(lf, axis=1)
        rel = jnp.where(causal, g[:, :, None] - g[:, None], NEG_BIG)
        a = jnp.einsum('bthk,bshk,btshk->bhts', qc, kc, jnp.exp(rel))
        o = (jnp.einsum('bthk,bhkv->bthv', qc * jnp.exp(g), s)
             + jnp.einsum('bhts,bshv->bthv', a, vc))
        g_last = g[:, -1]
        s_new = (jnp.exp(g_last)[..., None] * s
                 + jnp.einsum('bshk,bshv->bhkv', kc * jnp.exp(g_last[:, None] - g), vc))
        return s_new, o

    xs = tuple(_to_chunks(z, c) for z in (q, k, v, logf))
    s_fin, o = lax.scan(step, s0.astype(jnp.float32), xs)
    return _from_chunks(o), s_fin


def _mlstm_scan(q, k, v, logi, logf, c0, n0, m0):
    t = q.shape[1]
    c = math.gcd(t, B_CHUNK)
    causal = jnp.tril(jnp.ones((c, c), dtype=bool))[None, :, :, None]

    def step(carry, inp):
        cm, nv, m = carry
        qc, kc, vc, ic, fc = inp
        fcum = jnp.cumsum(fc, axis=1)
        dmat = jnp.where(causal, fcum[:, :, None] - fcum[:, None] + ic[:, None], NEG_BIG)
        b_inter = fcum + m[:, None]
        m_t = jnp.maximum(b_inter, dmat.max(axis=2))
        w_inter = jnp.exp(b_inter - m_t)
        pw = jnp.einsum('bthk,bshk->btsh', qc, kc) * jnp.exp(dmat - m_t[:, :, None])
        num = (w_inter[..., None] * jnp.einsum('bthk,bhkv->bthv', qc, cm)
               + jnp.einsum('btsh,bshv->bthv', pw, vc))
        den = w_inter * jnp.einsum('bthk,bhk->bth', qc, nv) + pw.sum(axis=2)
        h_out = num / jnp.maximum(jnp.abs(den), jnp.exp(-m_t))[..., None]
        m_new = m_t[:, -1]
        decay = jnp.exp(fcum[:, -1] + m - m_new)
        w_st = jnp.exp(fcum[:, -1:] - fcum + ic - m_new[:, None])
        cm_new = decay[..., None, None] * cm + jnp.einsum('bsh,bshk,bshv->bhkv', w_st, kc, vc)
        nv_new = decay[..., None] * nv + jnp.einsum('bsh,bshk->bhk', w_st, kc)
        return (cm_new, nv_new, m_new), h_out

    f32 = jnp.float32
    xs = tuple(_to_chunks(z, c) for z in (q, k, v, logi, logf))
    (c_f, n_f, m_f), h = lax.scan(step, (c0.astype(f32), n0.astype(f32), m0.astype(f32)), xs)
    return _from_chunks(h), c_f, n_f, m_f


def _fox_prompt(q, k, v, logf):
    bsz, t, h, d = q.shape
    nblk = t // Q_BLOCK
    fcum = jnp.cumsum(logf, axis=1).transpose(0, 2, 1)
    kpos = jnp.arange(t)

    def one_block(args):
        qi, fi, bidx = args
        qpos = bidx * Q_BLOCK + jnp.arange(Q_BLOCK)
        s = (jnp.einsum('bqhd,bkhd->bhqk', qi, k).astype(jnp.float32) * C_SCALE
             + fi[..., None] - fcum[:, :, None, :])
        s = jnp.where(kpos[None, :] <= qpos[:, None], s, NEG_BIG)
        p = jax.nn.softmax(s, axis=-1)
        return jnp.einsum('bhqk,bkhd->bqhd', p.astype(v.dtype), v)

    qb = q.reshape(bsz, nblk, Q_BLOCK, h, d).swapaxes(0, 1)
    fb = fcum.reshape(bsz, h, nblk, Q_BLOCK).transpose(2, 0, 1, 3)
    o = lax.map(one_block, (qb, fb, jnp.arange(nblk)))
    return o.swapaxes(0, 1).reshape(bsz, t, h, d)


def _fox_sample(q, k, v, logf, k_past, v_past, logf_past):
    t = q.shape[1]
    p_len = k_past.shape[1]
    lp = logf_past.astype(jnp.float32)
    r_past = (lax.cumsum(lp, axis=1, reverse=True) - lp).transpose(0, 2, 1)
    g_new = jnp.cumsum(logf, axis=1).transpose(0, 2, 1)
    s_past = (jnp.einsum('bqhd,bkhd->bhqk', q, k_past).astype(jnp.float32) * C_SCALE
              + g_new[..., None] + r_past[:, :, None, :])
    s_new = (jnp.einsum('bqhd,bkhd->bhqk', q, k).astype(jnp.float32) * C_SCALE
             + g_new[..., None] - g_new[:, :, None, :])
    s_new = jnp.where(jnp.tril(jnp.ones((t, t), dtype=bool)), s_new, NEG_BIG)
    p = jax.nn.softmax(jnp.concatenate([s_past, s_new], axis=-1), axis=-1).astype(v.dtype)
    return (jnp.einsum('bhqk,bkhd->bqhd', p[..., :p_len], v_past)
            + jnp.einsum('bhqk,bkhd->bqhd', p[..., p_len:], v))


def _layer(x, lw, hgrn_s0, c0, n0, m0, fox_past):
    (f1n, f1w1, f1w3, f1w2, mixn, w_in, lb, hnorm, b_i, b_f, mnorm, fox_bf,
     wpa, wpb, wpc, w_out, f2n, f2w1, f2w3, f2w2) = lw
    f32 = jnp.float32
    x = x + 0.5 * _swiglu(_rmsnorm(x, f1n), f1w1, f1w3, f1w2)
    h = _rmsnorm(x, mixn)
    bsz, t, _ = h.shape
    (aq, af, ai, ag, bq, bk, bv, bo, bi, bfg, cq, ck, cv, cf, ga, gb, gc) = jnp.split(
        h @ w_in, IN_OFFSETS, axis=-1)

    qa = jax.nn.silu(aq).reshape(bsz, t, A_HEADS, A_DK)
    lbf = lb.astype(f32)
    f_a = lbf + (1.0 - lbf) * jax.nn.sigmoid(af.astype(f32))
    logf_a = jnp.log(jnp.maximum(f_a, TINY)).reshape(bsz, t, A_HEADS, A_DK)
    ka = (1.0 - f_a).reshape(bsz, t, A_HEADS, A_DK)
    va = ai.reshape(bsz, t, A_HEADS, A_DV)
    oa, s_a = _hgrn2_scan(qa, ka, va, logf_a, hgrn_s0)
    oa = _rmsnorm(oa, hnorm).reshape(bsz, t, A_WIDTH).astype(x.dtype) * jax.nn.silu(ag)

    qb = bq.reshape(bsz, t, B_HEADS, B_DK)
    kb = bk.reshape(bsz, t, B_HEADS, B_DK) * (B_DK ** -0.5)
    vb = bv.reshape(bsz, t, B_HEADS, B_DV)
    logi = _softcap(bi.astype(f32) + b_i.astype(f32))
    logf_b = jax.nn.log_sigmoid(_softcap(bfg.astype(f32) + b_f.astype(f32)))
    ob, c_b, n_b, m_b = _mlstm_scan(qb, kb, vb, logi, logf_b, c0, n0, m0)
    ob = _rmsnorm(ob, mnorm).reshape(bsz, t, B_WIDTH).astype(x.dtype) * jax.nn.sigmoid(bo)

    qc = cq.reshape(bsz, t, C_HEADS, C_DH)
    kc = ck.reshape(bsz, t, C_HEADS, C_DH)
    vc = cv.reshape(bsz, t, C_HEADS, C_DH)
    logf_c = jax.nn.log_sigmoid(cf.astype(f32) + fox_bf.astype(f32))
    if fox_past is None:
        oc = _fox_prompt(qc, kc, vc, logf_c)
    else:
        oc = _fox_sample(qc, kc, vc, logf_c, *fox_past)
    oc = oc.reshape(bsz, t, C_WIDTH).astype(x.dtype)

    merged = (jax.nn.sigmoid(ga) * (oa @ wpa) + jax.nn.sigmoid(gb) * (ob @ wpb)
              + jax.nn.sigmoid(gc) * (oc @ wpc))
    x = x + merged @ w_out
    x = x + 0.5 * _swiglu(_rmsnorm(x, f2n), f2w1, f2w3, f2w2)
    return x, (kc, vc, logf_c, s_a, c_b, n_b, m_b)


def setup_inputs(seed: int = 0) -> dict:
    key = jax.random.key(seed)
    ks = iter(jax.random.split(key, 48))
    f32 = jnp.float32

    def nrm(shape, scale=1.0):
        return jax.random.normal(next(ks), shape, f32) * scale

    def gain(shape):
        return 1.0 + 0.1 * nrm(shape)

    n_pages = PAST_LEN // PAGE_SIZE
    n_used = DEC_BATCH * n_pages
    n_pool = n_used + max(1, n_used // 4)
    page_table = jax.random.permutation(next(ks), n_pool)[:n_used].reshape(DEC_BATCH, n_pages).astype(jnp.int32)

    inp = {}
    inp['x_prompt'] = nrm((BATCH, SEQ, D_MODEL))
    inp['x_sample'] = nrm((DEC_BATCH, DEC_SEQ, D_MODEL))
    inp['cache_k'] = nrm((DEPTH, n_pool, PAGE_SIZE, C_HEADS, C_DH))
    inp['cache_v'] = nrm((DEPTH, n_pool, PAGE_SIZE, C_HEADS, C_DH))
    inp['cache_logf'] = jax.nn.log_sigmoid(3.0 + nrm((DEPTH, n_pool, PAGE_SIZE, C_HEADS)))
    inp['state_hgrn'] = nrm((DEPTH, DEC_BATCH, A_HEADS, A_DK, A_DV))
    inp['state_mlstm_c'] = nrm((DEPTH, DEC_BATCH, B_HEADS, B_DK, B_DV))
    inp['state_mlstm_n'] = nrm((DEPTH, DEC_BATCH, B_HEADS, B_DK))
    inp['state_mlstm_m'] = nrm((DEPTH, DEC_BATCH, B_HEADS))
    inp['page_table'] = page_table
    inp['ffn1_norm'] = gain((DEPTH, D_MODEL))
    inp['ffn1_w1'] = nrm((DEPTH, D_MODEL, D_FF), D_MODEL ** -0.5)
    inp['ffn1_w3'] = nrm((DEPTH, D_MODEL, D_FF), D_MODEL ** -0.5)
    inp['ffn1_w2'] = nrm((DEPTH, D_FF, D_MODEL), D_FF ** -0.5)
    inp['mix_norm'] = gain((DEPTH, D_MODEL))
    inp['w_in'] = nrm((DEPTH, D_MODEL, IN_COLS), D_MODEL ** -0.5)
    inp['hgrn_lb'] = nrm((DEPTH, A_WIDTH))
    inp['hgrn_norm'] = gain((DEPTH, A_DV))
    inp['mlstm_b_i'] = 0.1 * nrm((DEPTH, B_HEADS))
    inp['mlstm_b_f'] = 3.0 + 0.1 * nrm((DEPTH, B_HEADS))
    inp['mlstm_norm'] = gain((DEPTH, B_DV))
    inp['fox_b_f'] = 3.0 + 0.1 * nrm((DEPTH, C_HEADS))
    inp['w_proj_a'] = nrm((DEPTH, A_WIDTH, D_MODEL), A_WIDTH ** -0.5)
    inp['w_proj_b'] = nrm((DEPTH, B_WIDTH, D_MODEL), B_WIDTH ** -0.5)
    inp['w_proj_c'] = nrm((DEPTH, C_WIDTH, D_MODEL), C_WIDTH ** -0.5)
    inp['w_out'] = nrm((DEPTH, D_MODEL, D_MODEL), D_MODEL ** -0.5)
    inp['ffn2_norm'] = gain((DEPTH, D_MODEL))
    inp['ffn2_w1'] = nrm((DEPTH, D_MODEL, D_FF), D_MODEL ** -0.5)
    inp['ffn2_w3'] = nrm((DEPTH, D_MODEL, D_FF), D_MODEL ** -0.5)
    inp['ffn2_w2'] = nrm((DEPTH, D_FF, D_MODEL), D_FF ** -0.5)
    inp['final_norm'] = gain((D_MODEL,))
    return inp


def reference(x_prompt, x_sample, cache_k, cache_v, cache_logf, state_hgrn, state_mlstm_c,
              state_mlstm_n, state_mlstm_m, page_table, ffn1_norm, ffn1_w1, ffn1_w3, ffn1_w2,
              mix_norm, w_in, hgrn_lb, hgrn_norm, mlstm_b_i, mlstm_b_f, mlstm_norm, fox_b_f,
              w_proj_a, w_proj_b, w_proj_c, w_out, ffn2_norm, ffn2_w1, ffn2_w3, ffn2_w2, final_norm):
    f32 = jnp.float32
    lb_w = jax.nn.softmax(hgrn_lb.astype(f32), axis=0)
    lb_all = jnp.concatenate([jnp.zeros_like(lb_w[:1]), jnp.cumsum(lb_w, axis=0)[:-1]], axis=0)
    lb_all = jnp.clip(lb_all, 0.0, 1.0)

    bp = x_prompt.shape[0]
    bs = page_table.shape[0]
    z_hgrn = jnp.zeros((bp, A_HEADS, A_DK, A_DV), f32)
    z_c = jnp.zeros((bp, B_HEADS, B_DK, B_DV), f32)
    z_n = jnp.zeros((bp, B_HEADS, B_DK), f32)
    z_m = jnp.zeros((bp, B_HEADS), f32)

    def gather(pool):
        return pool[page_table].reshape((bs, -1) + pool.shape[2:])

    yp, ys = x_prompt, x_sample
    p_states = [[] for _ in range(7)]
    s_states = [[] for _ in range(7)]
    for l in range(DEPTH):
        lw = (ffn1_norm[l], ffn1_w1[l], ffn1_w3[l], ffn1_w2[l], mix_norm[l], w_in[l], lb_all[l],
              hgrn_norm[l], mlstm_b_i[l], mlstm_b_f[l], mlstm_norm[l], fox_b_f[l],
              w_proj_a[l], w_proj_b[l], w_proj_c[l], w_out[l],
              ffn2_norm[l], ffn2_w1[l], ffn2_w3[l], ffn2_w2[l])
        yp, st_p = _layer(yp, lw, z_hgrn, z_c, z_n, z_m, None)
        past = (gather(cache_k[l]), gather(cache_v[l]), gather(cache_logf[l]))
        ys, st_s = _layer(ys, lw, state_hgrn[l], state_mlstm_c[l], state_mlstm_n[l],
                          state_mlstm_m[l], past)
        for i in range(7):
            p_states[i].append(st_p[i])
            s_states[i].append(st_s[i])

    yp = _rmsnorm(yp, final_norm)
    ys = _rmsnorm(ys, final_norm)
    pk, pv, plf, ph, pc, pn, pm = [jnp.stack(s, axis=0) for s in p_states]
    sk, sv, slf, sh, sc, sn, sm = [jnp.stack(s, axis=0) for s in s_states]
    return (yp, ys, pk, pv, plf, ph, pc, pn, pm, sk, sv, slf, sh, sc, sn, sm)
```

```python
import functools

import jax
import jax.numpy as jnp
from jax import lax
from jax.experimental import pallas as pl
from jax.experimental.pallas import tpu as pltpu

F32 = jnp.float32
BF16 = jnp.bfloat16

A_HEADS, A_DK, A_DV = 8, 128, 128
B_HEADS, B_DK, B_DV = 4, 128, 256
C_HEADS, C_DH = 16, 128
A_WIDTH = A_HEADS * A_DV
B_WIDTH = B_HEADS * B_DV
C_WIDTH = C_HEADS * C_DH
GATE_SOFTCAP = 15.0
C_SCALE = C_DH ** -0.5
EPS = 1e-6
NEG_BIG = -1e30
TINY = 1e-30
PAGE = 128

LANES = 128
VMEM_LIMIT = 56 * 1024 * 1024

_O_B = 4 * A_WIDTH
_O_BI = _O_B + 2 * B_HEADS * B_DK + 2 * B_WIDTH
_O_BF = _O_BI + B_HEADS
_O_C = _O_BF + B_HEADS
_O_CF = _O_C + 3 * C_WIDTH
_O_G = _O_CF + C_HEADS

SMALL_W = 512
ZB_W = 2 * B_HEADS * B_DK + 2 * B_WIDTH + SMALL_W
SMALL_OFF = ZB_W - SMALL_W
LANE_I = C_HEADS
LANE_F = C_HEADS + B_HEADS


def _cp(*sem):
    return pltpu.CompilerParams(dimension_semantics=sem, vmem_limit_bytes=VMEM_LIMIT)


def _tile(n, pref, mult=8):
    if n <= pref:
        return n
    t = (pref // mult) * mult
    while t >= mult:
        if n % t == 0:
            return t
        t -= mult
    return n


def _sigmoid(x):
    return jax.nn.sigmoid(x)


def _log_sigmoid(x):
    return jnp.minimum(x, 0.0) - jnp.log(1.0 + jnp.exp(-jnp.abs(x)))


def _dot(a, b):
    return jnp.dot(a, b, preferred_element_type=F32)


def _dot_nt(a, b):
    return lax.dot_general(a, b, (((1,), (1,)), ((), ())), preferred_element_type=F32)


def _dot_tn(a, b):
    return lax.dot_general(a, b, (((0,), (0,)), ((), ())), preferred_element_type=F32)


def _split3(x):
    hi = x.astype(BF16)
    r1 = x - hi.astype(F32)
    mid = r1.astype(BF16)
    lo = (r1 - mid.astype(F32)).astype(BF16)
    return hi, mid, lo


def _dot01_left(m01, x):
    hi, mid, lo = _split3(x)
    return (_dot(m01, hi) + _dot(m01, mid)) + _dot(m01, lo)


def _dot01_right(x, m01):
    hi, mid, lo = _split3(x)
    return (_dot(hi, m01) + _dot(mid, m01)) + _dot(lo, m01)


def _tri_incl(n):
    r = lax.broadcasted_iota(jnp.int32, (n, n), 0)
    c = lax.broadcasted_iota(jnp.int32, (n, n), 1)
    return r >= c


def _cumsum_rows(x):
    n = x.shape[0]
    return _dot01_left(_tri_incl(n).astype(BF16), x)


def _lane_col(x, lane):
    idx = lax.broadcasted_iota(jnp.int32, x.shape, 1)
    return jnp.sum(jnp.where(idx == lane, x, 0.0), axis=-1, keepdims=True)


def _col_to_row(col):
    n = col.shape[0]
    r = lax.broadcasted_iota(jnp.int32, (n, n), 0)
    c = lax.broadcasted_iota(jnp.int32, (n, n), 1)
    return jnp.sum(jnp.where(r == c, jnp.broadcast_to(col, (n, n)), 0.0), axis=0, keepdims=True)


def _lb_kernel(x_ref, o_ref):
    x = x_ref[...]
    e = jnp.exp(x - jnp.max(x, axis=0, keepdims=True))
    w = e / jnp.sum(e, axis=0, keepdims=True)
    run = jnp.zeros_like(w[0:1])
    for l in range(x.shape[0]):
        o_ref[l:l + 1, :] = jnp.clip(run, 0.0, 1.0)
        run = run + w[l:l + 1]


def _lower_bounds(hgrn_lb):
    return pl.pallas_call(
        _lb_kernel, out_shape=jax.ShapeDtypeStruct(hgrn_lb.shape, F32), name="hgrn_lower_bounds",
    )(hgrn_lb.astype(F32))


def _rmsnorm_kernel(x_ref, g_ref, o_ref):
    x = x_ref[...]
    y = x * lax.rsqrt(jnp.mean(x * x, axis=-1, keepdims=True) + EPS)
    o_ref[...] = (y * g_ref[...]).astype(o_ref.dtype)


def _rmsnorm(x, g, out_dtype):
    m, d = x.shape
    tm = _tile(m, 256)
    return pl.pallas_call(
        _rmsnorm_kernel,
        grid=(m // tm,),
        in_specs=[pl.BlockSpec((tm, d), lambda i: (i, 0)), pl.BlockSpec((1, d), lambda i: (0, 0))],
        out_specs=pl.BlockSpec((tm, d), lambda i: (i, 0)),
        out_shape=jax.ShapeDtypeStruct((m, d), out_dtype),
        compiler_params=_cp("arbitrary"),
        name="rmsnorm",
    )(x, g.reshape(1, d))


def _ffn_up_kernel(h_ref, w1_ref, w3_ref, o_ref, w1b, w3b):
    @pl.when(pl.program_id(1) == 0)
    def _():
        w1b[...] = w1_ref[...].astype(BF16)
        w3b[...] = w3_ref[...].astype(BF16)

    h = h_ref[...]
    a = _dot(h, w1b[...])
    b = _dot(h, w3b[...])
    o_ref[...] = (a * _sigmoid(a) * b).astype(o_ref.dtype)


def _ffn_up(h, w1, w3, l):
    m, d = h.shape
    f = w1.shape[-1]
    tm = _tile(m, 1024, 16)
    tn = _tile(f, 256, LANES)
    return pl.pallas_call(
        _ffn_up_kernel,
        grid=(f // tn, m // tm),
        in_specs=[pl.BlockSpec((tm, d), lambda j, i: (i, 0)),
                  pl.BlockSpec((None, d, tn), lambda j, i: (l, 0, j)),
                  pl.BlockSpec((None, d, tn), lambda j, i: (l, 0, j))],
        out_specs=pl.BlockSpec((tm, tn), lambda j, i: (i, j)),
        out_shape=jax.ShapeDtypeStruct((m, f), BF16),
        scratch_shapes=[pltpu.VMEM((d, tn), BF16), pltpu.VMEM((d, tn), BF16)],
        compiler_params=_cp("arbitrary", "arbitrary"),
        name="ffn_up",
    )(h, w1, w3)


def _down_kernel(a_ref, w_ref, r_ref, o_ref, acc_ref, *, scale, nk):
    k = pl.program_id(2)
    p = _dot(a_ref[...], w_ref[...].astype(BF16))

    @pl.when(k == 0)
    def _():
        acc_ref[...] = p

    @pl.when(k > 0)
    def _():
        acc_ref[...] += p

    @pl.when(k == nk - 1)
    def _():
        o_ref[...] = r_ref[...] + scale * acc_ref[...]


def _down(a, w, res, l, scale):
    m, kdim = a.shape
    n = w.shape[-1]
    tm = _tile(m, 1024, 16)
    tn = _tile(n, 256, LANES)
    tk = kdim if kdim <= 4096 else kdim // 2
    assert kdim % tk == 0 and tk % LANES == 0
    nk = kdim // tk
    return pl.pallas_call(
        functools.partial(_down_kernel, scale=scale, nk=nk),
        grid=(n // tn, m // tm, nk),
        in_specs=[pl.BlockSpec((tm, tk), lambda j, i, k: (i, k)),
                  pl.BlockSpec((None, tk, tn), lambda j, i, k: (l, k, j)),
                  pl.BlockSpec((tm, tn), lambda j, i, k: (i, j))],
        out_specs=pl.BlockSpec((tm, tn), lambda j, i, k: (i, j)),
        out_shape=jax.ShapeDtypeStruct((m, n), F32),
        scratch_shapes=[pltpu.VMEM((tm, tn), F32)],
        compiler_params=_cp("arbitrary", "arbitrary", "arbitrary"),
        name="down_residual",
    )(a, w, res)


def _proj_kernel(h_ref, w_ref, o_ref):
    o_ref[...] = _dot(h_ref[...], w_ref[...]).astype(o_ref.dtype)


def _proj(h, wp, l, col_off, ncols, out_dtype):
    m, d = h.shape
    tm = _tile(m, 1024, 16)
    tn = 512
    assert col_off % tn == 0 and ncols % tn == 0
    cb = col_off // tn
    return pl.pallas_call(
        _proj_kernel,
        grid=(ncols // tn, m // tm),
        in_specs=[pl.BlockSpec((tm, d), lambda j, i: (i, 0)),
                  pl.BlockSpec((None, d, tn), lambda j, i: (l, 0, cb + j))],
        out_specs=pl.BlockSpec((tm, tn), lambda j, i: (i, j)),
        out_shape=jax.ShapeDtypeStruct((m, ncols), out_dtype),
        compiler_params=_cp("arbitrary", "arbitrary"),
        name="in_proj",
    )(h, wp)


def _gates_kernel(s_ref, b_ref, act_ref, cum_ref, *rest, emit_rows):
    if emit_rows:
        row_ref, carry = rest
    else:
        (carry,) = rest

    @pl.when(pl.program_id(1) == 0)
    def _():
        carry[...] = jnp.zeros_like(carry)

    x = s_ref[...] + b_ref[...]
    lane = lax.broadcasted_iota(jnp.int32, x.shape, 1)
    capped = GATE_SOFTCAP * jnp.tanh(x / GATE_SOFTCAP)
    ls = _log_sigmoid(jnp.where(lane < LANE_I, x, capped))
    act = jnp.where((lane >= LANE_I) & (lane < LANE_F), capped, ls)
    act = jnp.where(lane < LANE_F + B_HEADS, act, 0.0)
    act_ref[...] = act
    c = _cumsum_rows(act) + carry[...]
    cum_ref[...] = c
    carry[...] = c[c.shape[0] - 1:, :]
    if emit_rows:
        row_ref[...] = c.T


def _gates(zb, bias, bsz, t, emit_rows):
    m = bsz * t
    tb = _tile(t, LANES)
    nb = t // tb
    sb = SMALL_OFF // LANES
    out_shape = [jax.ShapeDtypeStruct((m, LANES), F32), jax.ShapeDtypeStruct((m, LANES), F32)]
    out_specs = [pl.BlockSpec((tb, LANES), lambda b, j: (b * nb + j, 0)),
                 pl.BlockSpec((tb, LANES), lambda b, j: (b * nb + j, 0))]
    if emit_rows:
        out_shape.append(jax.ShapeDtypeStruct((bsz, LANES, t), F32))
        out_specs.append(pl.BlockSpec((None, LANES, tb), lambda b, j: (b, 0, j)))
    return pl.pallas_call(
        functools.partial(_gates_kernel, emit_rows=emit_rows),
        grid=(bsz, nb),
        in_specs=[pl.BlockSpec((tb, LANES), lambda b, j: (b * nb + j, sb)),
                  pl.BlockSpec((1, LANES), lambda b, j: (0, 0))],
        out_specs=out_specs,
        out_shape=out_shape,
        scratch_shapes=[pltpu.VMEM((1, LANES), F32)],
        compiler_params=_cp("arbitrary", "arbitrary"),
        name="gate_activations",
    )(zb, bias)


def _hgrn_chunk(aq, af, ai, ag, lb, hn, s_prev, sub):
    c_len = aq.shape[0]
    q = aq * _sigmoid(aq)
    f = lb + (1.0 - lb) * _sigmoid(af)
    lf = jnp.log(jnp.maximum(f, TINY))
    k = 1.0 - f
    v = ai
    g = _cumsum_rows(lf)
    o = _dot((q * jnp.exp(g)).astype(BF16), s_prev.astype(BF16))
    rows = lax.broadcasted_iota(jnp.int32, (sub, A_DK), 0)
    parts = []
    for i0 in range(0, c_len, sub):
        g_i = g[i0:i0 + sub]
        q_i = q[i0:i0 + sub]
        o_i = o[i0:i0 + sub]
        if i0 > 0:
            r = g[i0 - 1:i0]
            q_t = (q_i * jnp.exp(g_i - r)).astype(BF16)
            k_t = (k[:i0] * jnp.exp(r - g[:i0])).astype(BF16)
            a = _dot_nt(q_t, k_t)
            o_i = o_i + _dot(a.astype(BF16), v[:i0].astype(BF16))
        for s in range(sub):
            e = jnp.exp(jnp.minimum(g_i - g_i[s:s + 1], 0.0))
            w = jnp.where(rows >= s, q_i * (k[i0 + s:i0 + s + 1] * e), 0.0)
            o_i = o_i + jnp.sum(w, axis=-1, keepdims=True) * v[i0 + s:i0 + s + 1]
        parts.append(o_i)
    o = parts[0] if len(parts) == 1 else jnp.concatenate(parts, axis=0)
    g_last = g[c_len - 1:c_len]
    k_dec = (k * jnp.exp(g_last - g)).astype(BF16)
    dec_col = _row_to_col(jnp.exp(g_last))
    s_new = dec_col * s_prev + _dot_tn(k_dec, v.astype(BF16))
    y = o * lax.rsqrt(jnp.mean(o * o, axis=-1, keepdims=True) + EPS) * hn
    return y * (ag * _sigmoid(ag)), s_new


def _row_to_col(row):
    n = row.shape[1]
    r = lax.broadcasted_iota(jnp.int32, (n, n), 0)
    c = lax.broadcasted_iota(jnp.int32, (n, n), 1)
    return jnp.sum(jnp.where(r == c, jnp.broadcast_to(row, (n, n)), 0.0), axis=1, keepdims=True)


def _hgrn_kernel(aq_ref, af_ref, ai_ref, ag_ref, lb_ref, hn_ref, s0_ref, o_ref, sfin_ref, s_scr,
                 *, chunk, sub, nchunks, nblocks):
    tb = pl.program_id(2)

    @pl.when(tb == 0)
    def _():
        s_scr[...] = s0_ref[...]

    lb = lb_ref[...]
    hn = hn_ref[...]

    def body(ci, carry):
        r0 = pl.multiple_of(ci * chunk, chunk)
        sl = pl.ds(r0, chunk)
        out, s_new = _hgrn_chunk(aq_ref[sl, :], af_ref[sl, :], ai_ref[sl, :], ag_ref[sl, :],
                                 lb, hn, s_scr[...], sub)
        s_scr[...] = s_new
        o_ref[sl, :] = out
        return carry

    lax.fori_loop(0, nchunks, body, 0)

    @pl.when(tb == nblocks - 1)
    def _():
        sfin_ref[...] = s_scr[...]


def _hgrn(za, lb, hnorm, s0, bsz, t):
    chunk = min(t, 64)
    sub = min(chunk, 16)
    tg = _tile(t, 256, chunk)
    nblocks = t // tg

    def seg(s):
        return pl.BlockSpec((tg, A_DK), lambda b, h, j: (b * nblocks + j, s * A_HEADS + h))

    return pl.pallas_call(
        functools.partial(_hgrn_kernel, chunk=chunk, sub=sub, nchunks=tg // chunk, nblocks=nblocks),
        grid=(bsz, A_HEADS, nblocks),
        in_specs=[seg(0), seg(1), seg(2), seg(3),
                  pl.BlockSpec((1, A_DK), lambda b, h, j: (0, h)),
                  pl.BlockSpec((1, A_DV), lambda b, h, j: (0, 0)),
                  pl.BlockSpec((None, None, A_DK, A_DV), lambda b, h, j: (b, h, 0, 0))],
        out_specs=[pl.BlockSpec((tg, A_DV), lambda b, h, j: (b * nblocks + j, h)),
                   pl.BlockSpec((None, None, A_DK, A_DV), lambda b, h, j: (b, h, 0, 0))],
        out_shape=[jax.ShapeDtypeStruct((bsz * t, A_WIDTH), F32),
                   jax.ShapeDtypeStruct((bsz, A_HEADS, A_DK, A_DV), F32)],
        scratch_shapes=[pltpu.VMEM((A_DK, A_DV), F32)],
        compiler_params=_cp("arbitrary", "arbitrary", "arbitrary"),
        name="hgrn2",
    )(za, za, za, za, lb.reshape(1, A_WIDTH), hnorm.reshape(1, A_DV), s0)


def _mlstm_chunk(q, k, v, og, gact, head, mn, c_prev, n_prev, m_prev):
    c_len = q.shape[0]
    logi = _lane_col(gact, LANE_I + head)
    fcum = _lane_col(_cumsum_rows(gact), LANE_F + head)
    m0 = m_prev[:, 0:1]
    causal = _tri_incl(c_len)
    dmat = jnp.where(causal, fcum - _col_to_row(fcum) + _col_to_row(logi), NEG_BIG)
    b_inter = fcum + m0
    m_t = jnp.maximum(b_inter, jnp.max(dmat, axis=1, keepdims=True))
    w_inter = jnp.exp(b_inter - m_t)
    qb = q.astype(BF16)
    ks = k * (B_DK ** -0.5)
    pw = _dot_nt(qb, ks.astype(BF16)) * jnp.exp(dmat - m_t)
    num = w_inter * _dot(qb, c_prev.astype(BF16)) + _dot(pw.astype(BF16), v.astype(BF16))
    den = w_inter * jnp.sum(q * n_prev, axis=-1, keepdims=True) + jnp.sum(pw, axis=-1, keepdims=True)
    h_out = num / jnp.maximum(jnp.abs(den), jnp.exp(-m_t))
    f_last = fcum[c_len - 1:c_len]
    m_new = m_t[c_len - 1:c_len]
    decay = jnp.exp(f_last + m0 - m_new)
    w_st = jnp.exp(f_last - fcum + logi - m_new)
    kw = ks * w_st
    c_new = decay * c_prev + _dot_tn(kw.astype(BF16), v.astype(BF16))
    n_new = decay * n_prev + jnp.sum(kw, axis=0, keepdims=True)
    y = h_out * lax.rsqrt(jnp.mean(h_out * h_out, axis=-1, keepdims=True) + EPS) * mn
    return y * _sigmoid(og), c_new, n_new, jnp.broadcast_to(m_new, m_prev.shape)


def _mlstm_kernel(q_ref, k_ref, v_ref, og_ref, g_ref, mn_ref, c0_ref, n0_ref, m0_ref,
                  o_ref, cf_ref, nf_ref, mf_ref, c_scr, n_scr, m_scr, *, chunk, nchunks, nblocks):
    head = pl.program_id(1)
    tb = pl.program_id(2)

    @pl.when(tb == 0)
    def _():
        c_scr[...] = c0_ref[...]
        n_scr[...] = n0_ref[...]
        m_scr[...] = m0_ref[...]

    mn = mn_ref[...]

    def body(ci, carry):
        r0 = pl.multiple_of(ci * chunk, chunk)
        sl = pl.ds(r0, chunk)
        out, c_new, n_new, m_new = _mlstm_chunk(q_ref[sl, :], k_ref[sl, :], v_ref[sl, :], og_ref[sl, :],
                                                g_ref[sl, :], head, mn, c_scr[...], n_scr[...], m_scr[...])
        c_scr[...] = c_new
        n_scr[...] = n_new
        m_scr[...] = m_new
        o_ref[sl, :] = out
        return carry

    lax.fori_loop(0, nchunks, body, 0)

    @pl.when(tb == nblocks - 1)
    def _():
        cf_ref[...] = c_scr[...]
        nf_ref[...] = n_scr[...]
        mf_ref[...] = m_scr[...]


def _mlstm(zb, gact, mnorm, c0, n0, m0, bsz, t):
    chunk = min(t, 64)
    tg = _tile(t, 256, chunk)
    nblocks = t // tg
    nq = B_HEADS * B_DK // B_DK
    nv = 2 * B_HEADS * B_DK // B_DV
    row = lambda b, h, j: b * nblocks + j
    st4 = lambda b, h, j: (b, h, 0, 0)
    return pl.pallas_call(
        functools.partial(_mlstm_kernel, chunk=chunk, nchunks=tg // chunk, nblocks=nblocks),
        grid=(bsz, B_HEADS, nblocks),
        in_specs=[pl.BlockSpec((tg, B_DK), lambda b, h, j: (row(b, h, j), h)),
                  pl.BlockSpec((tg, B_DK), lambda b, h, j: (row(b, h, j), nq + h)),
                  pl.BlockSpec((tg, B_DV), lambda b, h, j: (row(b, h, j), nv + h)),
                  pl.BlockSpec((tg, B_DV), lambda b, h, j: (row(b, h, j), nv + B_HEADS + h)),
                  pl.BlockSpec((tg, LANES), lambda b, h, j: (row(b, h, j), 0)),
                  pl.BlockSpec((1, B_DV), lambda b, h, j: (0, 0)),
                  pl.BlockSpec((None, None, B_DK, B_DV), st4),
                  pl.BlockSpec((None, None, 1, B_DK), st4),
                  pl.BlockSpec((None, None, 1, LANES), st4)],
        out_specs=[pl.BlockSpec((tg, B_DV), lambda b, h, j: (row(b, h, j), h)),
                   pl.BlockSpec((None, None, B_DK, B_DV), st4),
                   pl.BlockSpec((None, None, 1, B_DK), st4),
                   pl.BlockSpec((None, None, 1, LANES), st4)],
        out_shape=[jax.ShapeDtypeStruct((bsz * t, B_WIDTH), F32),
                   jax.ShapeDtypeStruct((bsz, B_HEADS, B_DK, B_DV), F32),
                   jax.ShapeDtypeStruct((bsz, B_HEADS, 1, B_DK), F32),
                   jax.ShapeDtypeStruct((bsz, B_HEADS, 1, LANES), F32)],
        scratch_shapes=[pltpu.VMEM((B_DK, B_DV), F32), pltpu.VMEM((1, B_DK), F32),
                        pltpu.VMEM((1, LANES), F32)],
        compiler_params=_cp("arbitrary", "arbitrary", "arbitrary"),
        name="mlstm",
    )(zb, zb, zb, zb, gact, mnorm.reshape(1, B_DV), c0,
      n0.reshape(bsz, B_HEADS, 1, B_DK),
      jnp.broadcast_to(m0[:, :, None, None], (bsz, B_HEADS, 1, LANES)))


def _fox_prompt_kernel(q_ref, k_ref, v_ref, frow_ref, fcol_ref, o_ref, *, tq, tk):
    head = pl.program_id(1)
    qi = pl.program_id(2)
    q = q_ref[...]
    fcol = _lane_col(fcol_ref[...], head)
    qpos = qi * tq + lax.broadcasted_iota(jnp.int32, (tq, tk), 0)
    kiota = lax.broadcasted_iota(jnp.int32, (tq, tk), 1)

    def body(j, carry):
        m, l, acc = carry
        k0 = pl.multiple_of(j * tk, tk)
        kj = k_ref[pl.ds(k0, tk), :].astype(BF16)
        vj = v_ref[pl.ds(k0, tk), :].astype(BF16)
        fr = frow_ref[pl.ds(head, 1), pl.ds(k0, tk)]
        s = _dot_nt(q, kj) * C_SCALE + fcol - fr
        s = jnp.where(k0 + kiota <= qpos, s, NEG_BIG)
        m_new = jnp.maximum(m, jnp.max(s, axis=-1, keepdims=True))
        alpha = jnp.exp(m - m_new)
        p = jnp.exp(s - m_new)
        l = alpha * l + jnp.sum(p, axis=-1, keepdims=True)
        acc = alpha * acc + _dot(p.astype(BF16), vj)
        return m_new, l, acc

    nkb = (qi * tq + tq + tk - 1) // tk
    init = (jnp.full((tq, 1), NEG_BIG, F32), jnp.zeros((tq, 1), F32), jnp.zeros((tq, C_DH), F32))
    m, l, acc = lax.fori_loop(0, nkb, body, init)
    o_ref[...] = acc / l


def _fox_prompt(cq, ck, cv, frow, fcum, bsz, t):
    tq = _tile(t, 256, LANES)
    tk = tq
    nq = t // tq
    return pl.pallas_call(
        functools.partial(_fox_prompt_kernel, tq=tq, tk=tk),
        grid=(bsz, C_HEADS, nq),
        in_specs=[pl.BlockSpec((tq, C_DH), lambda b, h, i: (b * nq + i, h)),
                  pl.BlockSpec((t, C_DH), lambda b, h, i: (b, h)),
                  pl.BlockSpec((t, C_DH), lambda b, h, i: (b, h)),
                  pl.BlockSpec((None, C_HEADS, t), lambda b, h, i: (b, 0, 0)),
                  pl.BlockSpec((tq, LANES), lambda b, h, i: (b * nq + i, 0))],
        out_specs=pl.BlockSpec((tq, C_DH), lambda b, h, i: (b * nq + i, h)),
        out_shape=jax.ShapeDtypeStruct((bsz * t, C_WIDTH), F32),
        compiler_params=_cp("arbitrary", "arbitrary", "arbitrary"),
        name="fox_prompt",
    )(cq, ck, cv, frow, fcum)


def _rep_rows(x, rep):
    return jnp.concatenate([jnp.broadcast_to(x[i:i + 1], (rep, x.shape[1])) for i in range(x.shape[0])], axis=0)


def _fox_sample_kernel(pt_ref, qbd_ref, gq_ref, gk_ref, kn_ref, vn_ref, *rest, pp, nsteps, tq):
    k_refs = rest[:pp]
    v_refs = rest[pp:2 * pp]
    lf_refs = rest[2 * pp:3 * pp]
    o_ref = rest[3 * pp]
    m_scr, l_scr, acc_scr, carry_scr = rest[3 * pp + 1:]
    j = pl.program_id(1)
    rows = C_HEADS * tq

    @pl.when(j == 0)
    def _():
        m_scr[...] = jnp.full_like(m_scr, NEG_BIG)
        l_scr[...] = jnp.zeros_like(l_scr)
        acc_scr[...] = jnp.zeros_like(acc_scr)
        carry_scr[...] = jnp.zeros_like(carry_scr)

    qbd = qbd_ref[...]
    gq = gq_ref[...]

    def update(s_list, v_list):
        m_old = m_scr[...]
        m_new = m_old
        for s in s_list:
            m_new = jnp.maximum(m_new, jnp.max(s, axis=-1, keepdims=True))
        alpha = jnp.exp(m_old - m_new)
        l_new = alpha * l_scr[...]
        pv = None
        for s, v in zip(s_list, v_list):
            p = jnp.exp(s - m_new)
            l_new = l_new + jnp.sum(p, axis=-1, keepdims=True)
            d = _dot(p.astype(BF16), v.astype(BF16))
            pv = d if pv is None else pv + d
        m_scr[...] = m_new
        l_scr[...] = l_new
        acc_scr[...] = alpha * acc_scr[...] + pv

    r_i = lax.broadcasted_iota(jnp.int32, (PAGE, PAGE), 0)
    c_i = lax.broadcasted_iota(jnp.int32, (PAGE, PAGE), 1)
    u01 = (r_i > c_i).astype(BF16)
    carry = carry_scr[...]
    s_list = []
    for u in range(pp):
        lft = lf_refs[u][...]
        r_past = _dot01_right(lft, u01) + carry
        carry = carry + jnp.sum(lft, axis=-1, keepdims=True)
        s = _dot_nt(qbd, k_refs[u][...].astype(BF16)) * C_SCALE + gq + _rep_rows(r_past, tq)
        s_list.append(s)
    carry_scr[...] = carry
    update(s_list, [v_refs[u][...] for u in range(pp)])

    @pl.when(j == nsteps - 1)
    def _():
        s = _dot_nt(qbd, kn_ref[...].astype(BF16)) * C_SCALE + gq - gk_ref[...]
        qpos = lax.broadcasted_iota(jnp.int32, (rows, PAGE), 0) % tq
        kpos = lax.broadcasted_iota(jnp.int32, (rows, PAGE), 1)
        s = jnp.where(kpos <= qpos, s, NEG_BIG)
        update([s], [vn_ref[...]])
        acc = acc_scr[...] / l_scr[...]
        for h in range(C_HEADS):
            o_ref[:, h * C_DH:(h + 1) * C_DH] = acc[h * tq:(h + 1) * tq, h * C_DH:(h + 1) * C_DH]


def _fox_sample(cq, ck, cv, fcum, cache_k4, cache_v4, cache_lft, page_table, l, bsz, t):
    n_pages = page_table.shape[1]
    pp = 4 if n_pages % 4 == 0 else 1
    nsteps = n_pages // pp
    rows = C_HEADS * t
    q4 = cq.reshape(bsz, t, C_HEADS, C_DH).transpose(0, 2, 1, 3)
    eye = jnp.eye(C_HEADS, dtype=cq.dtype)
    qbd = (q4[:, :, :, None, :] * eye[None, :, None, :, None]).reshape(bsz, rows, C_WIDTH)
    g = fcum[:, :C_HEADS].reshape(bsz, t, C_HEADS)
    gq = g.transpose(0, 2, 1).reshape(bsz, rows, 1)
    gk = jnp.broadcast_to(g.transpose(0, 2, 1)[:, :, None, :], (bsz, C_HEADS, t, t)).reshape(bsz, rows, t)
    gk = jnp.pad(gk, ((0, 0), (0, 0), (0, PAGE - t)))
    kn = jnp.pad(ck.reshape(bsz, t, C_WIDTH), ((0, 0), (0, PAGE - t), (0, 0)))
    vn = jnp.pad(cv.reshape(bsz, t, C_WIDTH), ((0, 0), (0, PAGE - t), (0, 0)))

    def page_spec(u, shape):
        return pl.BlockSpec((None, None) + shape,
                            lambda b, j, pt: (l, pt[b * n_pages + n_pages - 1 - (j * pp + u)], 0, 0))

    batch3 = lambda b, j, pt: (b, 0, 0)
    in_specs = [pl.BlockSpec((None, rows, C_WIDTH), batch3),
                pl.BlockSpec((None, rows, 1), batch3),
                pl.BlockSpec((None, rows, PAGE), batch3),
                pl.BlockSpec((None, PAGE, C_WIDTH), batch3),
                pl.BlockSpec((None, PAGE, C_WIDTH), batch3)]
    in_specs += [page_spec(u, (PAGE, C_WIDTH)) for u in range(pp)]
    in_specs += [page_spec(u, (PAGE, C_WIDTH)) for u in range(pp)]
    in_specs += [page_spec(u, (C_HEADS, PAGE)) for u in range(pp)]
    grid_spec = pltpu.PrefetchScalarGridSpec(
        num_scalar_prefetch=1,
        grid=(bsz, nsteps),
        in_specs=in_specs,
        out_specs=pl.BlockSpec((t, C_WIDTH), lambda b, j, pt: (b, 0)),
        scratch_shapes=[pltpu.VMEM((rows, 1), F32), pltpu.VMEM((rows, 1), F32),
                        pltpu.VMEM((rows, C_WIDTH), F32), pltpu.VMEM((C_HEADS, 1), F32)],
    )
    return pl.pallas_call(
        functools.partial(_fox_sample_kernel, pp=pp, nsteps=nsteps, tq=t),
        grid_spec=grid_spec,
        out_shape=jax.ShapeDtypeStruct((bsz * t, C_WIDTH), F32),
        compiler_params=_cp("arbitrary", "arbitrary"),
        name="fox_sample",
    )(page_table.reshape(-1), qbd, gq, gk, kn, vn,
      *([cache_k4] * pp), *([cache_v4] * pp), *([cache_lft] * pp))


def _merge_kernel(oa_ref, ob_ref, oc_ref, wa_ref, wb_ref, wc_ref, ga_ref, gb_ref, gc_ref, o_ref):
    pa = _dot(oa_ref[...].astype(BF16), wa_ref[...].astype(BF16))
    pb = _dot(ob_ref[...].astype(BF16), wb_ref[...].astype(BF16))
    pc = _dot(oc_ref[...].astype(BF16), wc_ref[...].astype(BF16))
    out = _sigmoid(ga_ref[...]) * pa + _sigmoid(gb_ref[...]) * pb + _sigmoid(gc_ref[...]) * pc
    o_ref[...] = out.astype(o_ref.dtype)


def _merge(oa, ob, oc, wpa, wpb, wpc, zg, l):
    m = oa.shape[0]
    d = wpa.shape[-1]
    tm = _tile(m, 1024, 16)
    tn = _tile(d, 256, LANES)
    nb = d // tn
    act = lambda w: pl.BlockSpec((tm, w), lambda j, i: (i, 0))
    wgt = lambda w: pl.BlockSpec((None, w, tn), lambda j, i: (l, 0, j))
    gate = lambda g: pl.BlockSpec((tm, tn), lambda j, i: (i, g * nb + j))
    return pl.pallas_call(
        _merge_kernel,
        grid=(nb, m // tm),
        in_specs=[act(A_WIDTH), act(B_WIDTH), act(C_WIDTH), wgt(A_WIDTH), wgt(B_WIDTH), wgt(C_WIDTH),
                  gate(0), gate(1), gate(2)],
        out_specs=pl.BlockSpec((tm, tn), lambda j, i: (i, j)),
        out_shape=jax.ShapeDtypeStruct((m, d), BF16),
        compiler_params=_cp("arbitrary", "arbitrary"),
        name="gated_merge",
    )(oa, ob, oc, wpa, wpb, wpc, zg, zg, zg)


def _pack_w_in(w_in):
    depth, d, _ = w_in.shape
    pad = jnp.zeros((depth, d, SMALL_W - C_HEADS - 2 * B_HEADS), w_in.dtype)
    packed = jnp.concatenate([w_in[:, :, _O_G:], w_in[:, :, :_O_BI], w_in[:, :, _O_CF:_O_G],
                              w_in[:, :, _O_BI:_O_C], pad, w_in[:, :, _O_C:_O_CF]], axis=-1)
    return packed.astype(BF16)


def _layer(x, l, w, bsz, t, states, past):
    (ffn1_norm, ffn1_w1, ffn1_w3, ffn1_w2, mix_norm, wp, lb_all, hgrn_norm, gate_bias, mlstm_norm,
     w_proj_a, w_proj_b, w_proj_c, w_out, ffn2_norm, ffn2_w1, ffn2_w3, ffn2_w2) = w
    d = x.shape[-1]
    s_a0, c0, n0, m0 = states

    h = _rmsnorm(x, ffn1_norm[l], BF16)
    x = _down(_ffn_up(h, ffn1_w1, ffn1_w3, l), ffn1_w2, x, l, 0.5)

    h = _rmsnorm(x, mix_norm[l], BF16)
    off_a = 3 * d
    off_b = off_a + 4 * A_WIDTH
    off_c = off_b + ZB_W
    zg = _proj(h, wp, l, 0, 3 * d, F32)
    za = _proj(h, wp, l, off_a, 4 * A_WIDTH, F32)
    zb = _proj(h, wp, l, off_b, ZB_W, F32)
    cq = _proj(h, wp, l, off_c, C_WIDTH, BF16)
    ck = _proj(h, wp, l, off_c + C_WIDTH, C_WIDTH, F32)
    cv = _proj(h, wp, l, off_c + 2 * C_WIDTH, C_WIDTH, F32)

    oa, s_a = _hgrn(za, lb_all[l], hgrn_norm[l], s_a0, bsz, t)
    if past is None:
        gact, fcum, frow = _gates(zb, gate_bias[l], bsz, t, True)
    else:
        gact, fcum = _gates(zb, gate_bias[l], bsz, t, False)
    ob, c_b, n_b, m_b = _mlstm(zb, gact, mlstm_norm[l], c0, n0, m0, bsz, t)
    if past is None:
        oc = _fox_prompt(cq, ck, cv, frow, fcum, bsz, t)
    else:
        oc = _fox_sample(cq, ck, cv, fcum, *past, l, bsz, t)

    merged = _merge(oa, ob, oc, w_proj_a, w_proj_b, w_proj_c, zg, l)
    x = _down(merged, w_out, x, l, 1.0)

    h = _rmsnorm(x, ffn2_norm[l], BF16)
    x = _down(_ffn_up(h, ffn2_w1, ffn2_w3, l), ffn2_w2, x, l, 0.5)

    st = (ck.reshape(bsz, t, C_HEADS, C_DH), cv.reshape(bsz, t, C_HEADS, C_DH),
          gact[:, :C_HEADS].reshape(bsz, t, C_HEADS), s_a, c_b,
          n_b.reshape(bsz, B_HEADS, B_DK), m_b[:, :, 0, 0])
    return x, st


def kernel(x_prompt, x_sample, cache_k, cache_v, cache_logf, state_hgrn, state_mlstm_c, state_mlstm_n,
           state_mlstm_m, page_table, ffn1_norm, ffn1_w1, ffn1_w3, ffn1_w2, mix_norm, w_in, hgrn_lb,
           hgrn_norm, mlstm_b_i, mlstm_b_f, mlstm_norm, fox_b_f, w_proj_a, w_proj_b, w_proj_c, w_out,
           ffn2_norm, ffn2_w1, ffn2_w3, ffn2_w2, final_norm):
    depth = w_in.shape[0]
    bp, tp, d = x_prompt.shape
    bs, ts, _ = x_sample.shape
    n_pool = cache_k.shape[1]

    lb_all = _lower_bounds(hgrn_lb)
    wp = _pack_w_in(w_in)
    gate_bias = jnp.concatenate(
        [fox_b_f, mlstm_b_i, mlstm_b_f, jnp.zeros((depth, LANES - C_HEADS - 2 * B_HEADS), F32)],
        axis=-1).reshape(depth, 1, LANES).astype(F32)
    w = (ffn1_norm, ffn1_w1, ffn1_w3, ffn1_w2, mix_norm, wp, lb_all, hgrn_norm, gate_bias, mlstm_norm,
         w_proj_a, w_proj_b, w_proj_c, w_out, ffn2_norm, ffn2_w1, ffn2_w3, ffn2_w2)

    cache_k4 = cache_k.reshape(depth, n_pool, PAGE, C_WIDTH)
    cache_v4 = cache_v.reshape(depth, n_pool, PAGE, C_WIDTH)
    cache_lft = jnp.swapaxes(cache_logf.astype(F32), -1, -2)

    zero_states = (jnp.zeros((bp, A_HEADS, A_DK, A_DV), F32), jnp.zeros((bp, B_HEADS, B_DK, B_DV), F32),
                   jnp.zeros((bp, B_HEADS, B_DK), F32), jnp.zeros((bp, B_HEADS), F32))

    yp = x_prompt.reshape(bp * tp, d)
    ys = x_sample.reshape(bs * ts, d)
    p_states = [[] for _ in range(7)]
    s_states = [[] for _ in range(7)]
    for l in range(depth):
        yp, st_p = _layer(yp, l, w, bp, tp, zero_states, None)
        ys, st_s = _layer(ys, l, w, bs, ts,
                          (state_hgrn[l], state_mlstm_c[l], state_mlstm_n[l], state_mlstm_m[l]),
                          (cache_k4, cache_v4, cache_lft, page_table))
        for i in range(7):
            p_states[i].append(st_p[i])
            s_states[i].append(st_s[i])

    yp = _rmsnorm(yp, final_norm, F32).reshape(bp, tp, d)
    ys = _rmsnorm(ys, final_norm, F32).reshape(bs, ts, d)
    outs_p = [jnp.stack(s, axis=0) for s in p_states]
    outs_s = [jnp.stack(s, axis=0) for s in s_states]
    return (yp, ys, *outs_p, *outs_s)
```

```python
import functools

import jax
import jax.numpy as jnp
from jax import lax
from jax.experimental import pallas as pl
from jax.experimental.pallas import tpu as pltpu

F32 = jnp.float32
BF16 = jnp.bfloat16

A_HEADS, A_DK, A_DV = 8, 128, 128
B_HEADS, B_DK, B_DV = 4, 128, 256
C_HEADS, C_DH = 16, 128
A_WIDTH = A_HEADS * A_DV
B_WIDTH = B_HEADS * B_DV
C_WIDTH = C_HEADS * C_DH
GATE_SOFTCAP = 15.0
C_SCALE = C_DH ** -0.5
EPS = 1e-6
NEG_BIG = -1e30
TINY = 1e-30
PAGE = 128

LANES = 128
VMEM_LIMIT = 56 * 1024 * 1024

_O_B = 4 * A_WIDTH
_O_BI = _O_B + 2 * B_HEADS * B_DK + 2 * B_WIDTH
_O_BF = _O_BI + B_HEADS
_O_C = _O_BF + B_HEADS
_O_CF = _O_C + 3 * C_WIDTH
_O_G = _O_CF + C_HEADS

ZB_W = 2 * B_HEADS * B_DK + 2 * B_WIDTH
LANE_I = C_HEADS
LANE_F = C_HEADS + B_HEADS
SCAN_HEADS_PER_STEP = 2


def _cp(*sem):
    return pltpu.CompilerParams(dimension_semantics=sem, vmem_limit_bytes=VMEM_LIMIT)


def _tile(n, pref, mult=8):
    if n <= pref:
        return n
    t = (pref // mult) * mult
    while t >= mult:
        if n % t == 0:
            return t
        t -= mult
    return n


def _act_dtype(rows):
    return BF16 if rows % 16 == 0 else F32


def _sigmoid(x):
    return jax.nn.sigmoid(x)


def _log_sigmoid(x):
    return jnp.minimum(x, 0.0) - jnp.log(1.0 + jnp.exp(-jnp.abs(x)))


def _dot(a, b):
    return jnp.dot(a, b, preferred_element_type=F32)


def _dot_nt(a, b):
    return lax.dot_general(a, b, (((1,), (1,)), ((), ())), preferred_element_type=F32)


def _dot_tn(a, b):
    return lax.dot_general(a, b, (((0,), (0,)), ((), ())), preferred_element_type=F32)


def _split3(x):
    hi = x.astype(BF16)
    r1 = x - hi.astype(F32)
    mid = r1.astype(BF16)
    lo = (r1 - mid.astype(F32)).astype(BF16)
    return hi, mid, lo


def _dot01_left(m01, x):
    hi, mid, lo = _split3(x)
    return (_dot(m01, hi) + _dot(m01, mid)) + _dot(m01, lo)


def _dot01_right(x, m01):
    hi, mid, lo = _split3(x)
    return (_dot(hi, m01) + _dot(mid, m01)) + _dot(lo, m01)


def _tri_incl(n):
    r = lax.broadcasted_iota(jnp.int32, (n, n), 0)
    c = lax.broadcasted_iota(jnp.int32, (n, n), 1)
    return r >= c


def _cumsum_rows(x):
    n = x.shape[0]
    return _dot01_left(_tri_incl(n).astype(BF16), x)


def _lane_col(x, lane):
    idx = lax.broadcasted_iota(jnp.int32, x.shape, 1)
    return jnp.sum(jnp.where(idx == lane, x, 0.0), axis=-1, keepdims=True)


def _col_to_row(col):
    n = col.shape[0]
    r = lax.broadcasted_iota(jnp.int32, (n, n), 0)
    c = lax.broadcasted_iota(jnp.int32, (n, n), 1)
    return jnp.sum(jnp.where(r == c, jnp.broadcast_to(col, (n, n)), 0.0), axis=0, keepdims=True)


def _lb_kernel(x_ref, o_ref):
    x = x_ref[...]
    e = jnp.exp(x - jnp.max(x, axis=0, keepdims=True))
    w = e / jnp.sum(e, axis=0, keepdims=True)
    run = jnp.zeros_like(w[0:1])
    for l in range(x.shape[0]):
        o_ref[l:l + 1, :] = jnp.clip(run, 0.0, 1.0)
        run = run + w[l:l + 1]


def _lower_bounds(hgrn_lb):
    return pl.pallas_call(
        _lb_kernel, out_shape=jax.ShapeDtypeStruct(hgrn_lb.shape, F32), name="hgrn_lower_bounds",
    )(hgrn_lb.astype(F32))


def _rmsnorm_kernel(x_ref, g_ref, o_ref):
    x = x_ref[...]
    y = x * lax.rsqrt(jnp.mean(x * x, axis=-1, keepdims=True) + EPS)
    o_ref[...] = (y * g_ref[...]).astype(o_ref.dtype)


def _rmsnorm(x, g, out_dtype):
    m, d = x.shape
    tm = _tile(m, 256)
    return pl.pallas_call(
        _rmsnorm_kernel,
        grid=(m // tm,),
        in_specs=[pl.BlockSpec((tm, d), lambda i: (i, 0)), pl.BlockSpec((1, d), lambda i: (0, 0))],
        out_specs=pl.BlockSpec((tm, d), lambda i: (i, 0)),
        out_shape=jax.ShapeDtypeStruct((m, d), out_dtype),
        compiler_params=_cp("arbitrary"),
        name="rmsnorm",
    )(x, g.reshape(1, d))


def _ffn_up_kernel(h_ref, w1_ref, w3_ref, o_ref, w1b, w3b):
    @pl.when(pl.program_id(1) == 0)
    def _():
        w1b[...] = w1_ref[...].astype(BF16)
        w3b[...] = w3_ref[...].astype(BF16)

    h = h_ref[...]
    a = _dot(h, w1b[...])
    b = _dot(h, w3b[...])
    o_ref[...] = (a * _sigmoid(a) * b).astype(o_ref.dtype)


def _ffn_up(h, w1, w3, l):
    m, d = h.shape
    f = w1.shape[-1]
    tm = _tile(m, 1024, 16)
    tn = _tile(f, 256, LANES)
    return pl.pallas_call(
        _ffn_up_kernel,
        grid=(f // tn, m // tm),
        in_specs=[pl.BlockSpec((tm, d), lambda j, i: (i, 0)),
                  pl.BlockSpec((None, d, tn), lambda j, i: (l, 0, j)),
                  pl.BlockSpec((None, d, tn), lambda j, i: (l, 0, j))],
        out_specs=pl.BlockSpec((tm, tn), lambda j, i: (i, j)),
        out_shape=jax.ShapeDtypeStruct((m, f), BF16),
        scratch_shapes=[pltpu.VMEM((d, tn), BF16), pltpu.VMEM((d, tn), BF16)],
        compiler_params=_cp("arbitrary", "arbitrary"),
        name="ffn_up",
    )(h, w1, w3)


def _down_kernel(a_ref, w_ref, r_ref, o_ref, *acc, scale, nk):
    k = pl.program_id(1)
    j = pl.program_id(2)
    p = _dot(a_ref[...], w_ref[...].astype(BF16))
    if nk == 1:
        o_ref[...] = r_ref[...] + scale * p
        return
    acc_ref = acc[0]

    @pl.when(k == 0)
    def _():
        acc_ref[j] = p

    @pl.when((k > 0) & (k < nk - 1))
    def _():
        acc_ref[j] += p

    @pl.when(k == nk - 1)
    def _():
        o_ref[...] = r_ref[...] + scale * (acc_ref[j] + p)


def _down(a, w, res, l, scale):
    m, kdim = a.shape
    n = w.shape[-1]
    tm = _tile(m, 1024, 16)
    tn = _tile(n, 256, LANES)
    tk = kdim if kdim <= 4096 else kdim // 2
    assert kdim % tk == 0 and tk % LANES == 0
    nk = kdim // tk
    nj = n // tn
    out_idx = lambda i, k, j: (i, jnp.where(k == nk - 1, j, 0))
    return pl.pallas_call(
        functools.partial(_down_kernel, scale=scale, nk=nk),
        grid=(m // tm, nk, nj),
        in_specs=[pl.BlockSpec((tm, tk), lambda i, k, j: (i, k)),
                  pl.BlockSpec((None, tk, tn), lambda i, k, j: (l, k, j)),
                  pl.BlockSpec((tm, tn), out_idx)],
        out_specs=pl.BlockSpec((tm, tn), out_idx),
        out_shape=jax.ShapeDtypeStruct((m, n), F32),
        scratch_shapes=[pltpu.VMEM((nj, tm, tn), F32)] if nk > 1 else [],
        compiler_params=_cp("arbitrary", "arbitrary", "arbitrary"),
        name="down_residual",
    )(a, w, res)


def _proj_kernel(h_ref, w_ref, o_ref, *wb):
    if wb:
        @pl.when(pl.program_id(1) == 0)
        def _():
            wb[0][...] = w_ref[...].astype(BF16)
        w = wb[0][...]
    else:
        w = w_ref[...]
    o_ref[...] = _dot(h_ref[...], w).astype(o_ref.dtype)


def _proj(h, w, l, col_off, ncols, out_dtype):
    m, d = h.shape
    tm = _tile(m, 1024, 16)
    tn = _tile(ncols, 512, LANES)
    assert col_off % tn == 0 and ncols % tn == 0
    cb = col_off // tn
    scratch = [pltpu.VMEM((d, tn), BF16)] if w.dtype != BF16 else []
    return pl.pallas_call(
        _proj_kernel,
        grid=(ncols // tn, m // tm),
        in_specs=[pl.BlockSpec((tm, d), lambda j, i: (i, 0)),
                  pl.BlockSpec((None, d, tn), lambda j, i: (l, 0, cb + j))],
        out_specs=pl.BlockSpec((tm, tn), lambda j, i: (i, j)),
        out_shape=jax.ShapeDtypeStruct((m, ncols), out_dtype),
        scratch_shapes=scratch,
        compiler_params=_cp("arbitrary", "arbitrary"),
        name="in_proj",
    )(h, w)


def _gates_kernel(s_ref, b_ref, act_ref, cum_ref, *rest, emit_rows):
    if emit_rows:
        row_ref, carry = rest
    else:
        (carry,) = rest

    @pl.when(pl.program_id(1) == 0)
    def _():
        carry[...] = jnp.zeros_like(carry)

    x = s_ref[...] + b_ref[...]
    lane = lax.broadcasted_iota(jnp.int32, x.shape, 1)
    capped = GATE_SOFTCAP * jnp.tanh(x / GATE_SOFTCAP)
    ls = _log_sigmoid(jnp.where(lane < LANE_I, x, capped))
    act = jnp.where((lane >= LANE_I) & (lane < LANE_F), capped, ls)
    act = jnp.where(lane < LANE_F + B_HEADS, act, 0.0)
    act_ref[...] = act
    c = _cumsum_rows(act) + carry[...]
    cum_ref[...] = c
    carry[...] = c[c.shape[0] - 1:, :]
    if emit_rows:
        row_ref[...] = c.T


def _gates(zs, bias, bsz, t, emit_rows):
    m = bsz * t
    tb = _tile(t, LANES)
    nb = t // tb
    out_shape = [jax.ShapeDtypeStruct((m, LANES), F32), jax.ShapeDtypeStruct((m, LANES), F32)]
    out_specs = [pl.BlockSpec((tb, LANES), lambda b, j: (b * nb + j, 0)),
                 pl.BlockSpec((tb, LANES), lambda b, j: (b * nb + j, 0))]
    if emit_rows:
        out_shape.append(jax.ShapeDtypeStruct((bsz, LANES, t), F32))
        out_specs.append(pl.BlockSpec((None, LANES, tb), lambda b, j: (b, 0, j)))
    return pl.pallas_call(
        functools.partial(_gates_kernel, emit_rows=emit_rows),
        grid=(bsz, nb),
        in_specs=[pl.BlockSpec((tb, LANES), lambda b, j: (b * nb + j, 0)),
                  pl.BlockSpec((1, LANES), lambda b, j: (0, 0))],
        out_specs=out_specs,
        out_shape=out_shape,
        scratch_shapes=[pltpu.VMEM((1, LANES), F32)],
        compiler_params=_cp("arbitrary", "arbitrary"),
        name="gate_activations",
    )(zs, bias)


def _hgrn_chunk(aq, af, ai, ag, lb, hn, s_prev, sub):
    c_len = aq.shape[0]
    q = aq * _sigmoid(aq)
    f = lb + (1.0 - lb) * _sigmoid(af)
    lf = jnp.log(jnp.maximum(f, TINY))
    k = 1.0 - f
    v = ai
    g = _cumsum_rows(lf)
    o = _dot((q * jnp.exp(g)).astype(BF16), s_prev.astype(BF16))
    rows = lax.broadcasted_iota(jnp.int32, (sub, A_DK), 0)
    parts = []
    for i0 in range(0, c_len, sub):
        g_i = g[i0:i0 + sub]
        q_i = q[i0:i0 + sub]
        o_i = o[i0:i0 + sub]
        if i0 > 0:
            r = g[i0 - 1:i0]
            q_t = (q_i * jnp.exp(g_i - r)).astype(BF16)
            k_t = (k[:i0] * jnp.exp(r - g[:i0])).astype(BF16)
            a = _dot_nt(q_t, k_t)
            o_i = o_i + _dot(a.astype(BF16), v[:i0].astype(BF16))
        for s in range(sub):
            e = jnp.exp(jnp.minimum(g_i - g_i[s:s + 1], 0.0))
            w = jnp.where(rows >= s, q_i * (k[i0 + s:i0 + s + 1] * e), 0.0)
            o_i = o_i + jnp.sum(w, axis=-1, keepdims=True) * v[i0 + s:i0 + s + 1]
        parts.append(o_i)
    o = parts[0] if len(parts) == 1 else jnp.concatenate(parts, axis=0)
    g_last = g[c_len - 1:c_len]
    k_dec = (k * jnp.exp(g_last - g)).astype(BF16)
    dec_col = _row_to_col(jnp.exp(g_last))
    s_new = dec_col * s_prev + _dot_tn(k_dec, v.astype(BF16))
    y = o * lax.rsqrt(jnp.mean(o * o, axis=-1, keepdims=True) + EPS) * hn
    return y * (ag * _sigmoid(ag)), s_new


def _row_to_col(row):
    n = row.shape[1]
    r = lax.broadcasted_iota(jnp.int32, (n, n), 0)
    c = lax.broadcasted_iota(jnp.int32, (n, n), 1)
    return jnp.sum(jnp.where(r == c, jnp.broadcast_to(row, (n, n)), 0.0), axis=1, keepdims=True)


def _hgrn_kernel(aq_ref, af_ref, ai_ref, ag_ref, lb_ref, hn_ref, s0_ref, o_ref, sfin_ref, s_scr,
                 *, chunk, sub, nchunks, nblocks):
    tb = pl.program_id(2)

    @pl.when(tb == 0)
    def _():
        s_scr[...] = s0_ref[...]

    hn = hn_ref[...]

    def body(ci, carry):
        r0 = pl.multiple_of(ci * chunk, chunk)
        sl = pl.ds(r0, chunk)
        for hh in range(SCAN_HEADS_PER_STEP):
            cs = slice(hh * A_DK, (hh + 1) * A_DK)
            out, s_new = _hgrn_chunk(aq_ref[sl, cs], af_ref[sl, cs], ai_ref[sl, cs], ag_ref[sl, cs],
                                     lb_ref[:, cs], hn, s_scr[hh], sub)
            s_scr[hh] = s_new
            o_ref[sl, cs] = out.astype(o_ref.dtype)
        return carry

    lax.fori_loop(0, nchunks, body, 0)

    @pl.when(tb == nblocks - 1)
    def _():
        sfin_ref[...] = s_scr[...]


def _hgrn(za, lb, hnorm, s0, bsz, t):
    chunk = min(t, 64)
    sub = min(chunk, 16)
    tg = _tile(t, 256, chunk)
    nblocks = t // tg
    hp = SCAN_HEADS_PER_STEP
    ng = A_HEADS // hp

    def seg(s):
        return pl.BlockSpec((tg, hp * A_DK), lambda b, h, j: (b * nblocks + j, s * ng + h))

    state = pl.BlockSpec((None, hp, A_DK, A_DV), lambda b, h, j: (b, h, 0, 0))
    return pl.pallas_call(
        functools.partial(_hgrn_kernel, chunk=chunk, sub=sub, nchunks=tg // chunk, nblocks=nblocks),
        grid=(bsz, ng, nblocks),
        in_specs=[seg(0), seg(1), seg(2), seg(3),
                  pl.BlockSpec((1, hp * A_DK), lambda b, h, j: (0, h)),
                  pl.BlockSpec((1, A_DV), lambda b, h, j: (0, 0)),
                  state],
        out_specs=[pl.BlockSpec((tg, hp * A_DV), lambda b, h, j: (b * nblocks + j, h)), state],
        out_shape=[jax.ShapeDtypeStruct((bsz * t, A_WIDTH), _act_dtype(tg)),
                   jax.ShapeDtypeStruct((bsz, A_HEADS, A_DK, A_DV), F32)],
        scratch_shapes=[pltpu.VMEM((hp, A_DK, A_DV), F32)],
        compiler_params=_cp("arbitrary", "arbitrary", "arbitrary"),
        name="hgrn2",
    )(za, za, za, za, lb.reshape(1, A_WIDTH), hnorm.reshape(1, A_DV), s0)


def _mlstm_chunk(q, k, v, og, gact, head, mn, c_prev, n_prev, m_prev):
    c_len = q.shape[0]
    logi = _lane_col(gact, LANE_I + head)
    fcum = _lane_col(_cumsum_rows(gact), LANE_F + head)
    m0 = m_prev[:, 0:1]
    causal = _tri_incl(c_len)
    dmat = jnp.where(causal, fcum - _col_to_row(fcum) + _col_to_row(logi), NEG_BIG)
    b_inter = fcum + m0
    m_t = jnp.maximum(b_inter, jnp.max(dmat, axis=1, keepdims=True))
    w_inter = jnp.exp(b_inter - m_t)
    qb = q.astype(BF16)
    ks = k * (B_DK ** -0.5)
    pw = _dot_nt(qb, ks.astype(BF16)) * jnp.exp(dmat - m_t)
    num = w_inter * _dot(qb, c_prev.astype(BF16)) + _dot(pw.astype(BF16), v.astype(BF16))
    den = w_inter * jnp.sum(q * n_prev, axis=-1, keepdims=True) + jnp.sum(pw, axis=-1, keepdims=True)
    h_out = num / jnp.maximum(jnp.abs(den), jnp.exp(-m_t))
    f_last = fcum[c_len - 1:c_len]
    m_new = m_t[c_len - 1:c_len]
    decay = jnp.exp(f_last + m0 - m_new)
    w_st = jnp.exp(f_last - fcum + logi - m_new)
    kw = ks * w_st
    c_new = decay * c_prev + _dot_tn(kw.astype(BF16), v.astype(BF16))
    n_new = decay * n_prev + jnp.sum(kw, axis=0, keepdims=True)
    y = h_out * lax.rsqrt(jnp.mean(h_out * h_out, axis=-1, keepdims=True) + EPS) * mn
    return y * _sigmoid(og), c_new, n_new, jnp.broadcast_to(m_new, m_prev.shape)


def _mlstm_kernel(q_ref, k_ref, v_ref, og_ref, g_ref, mn_ref, c0_ref, n0_ref, m0_ref,
                  o_ref, cf_ref, nf_ref, mf_ref, c_scr, n_scr, m_scr, *, chunk, nchunks, nblocks):
    hp = SCAN_HEADS_PER_STEP
    hg = pl.program_id(1)
    tb = pl.program_id(2)

    @pl.when(tb == 0)
    def _():
        c_scr[...] = c0_ref[...]
        n_scr[...] = n0_ref[...]
        m_scr[...] = m0_ref[...]

    mn = mn_ref[...]

    def body(ci, carry):
        r0 = pl.multiple_of(ci * chunk, chunk)
        sl = pl.ds(r0, chunk)
        gact = g_ref[sl, :]
        for hh in range(hp):
            ks = slice(hh * B_DK, (hh + 1) * B_DK)
            vs = slice(hh * B_DV, (hh + 1) * B_DV)
            out, c_new, n_new, m_new = _mlstm_chunk(q_ref[sl, ks], k_ref[sl, ks], v_ref[sl, vs], og_ref[sl, vs],
                                                    gact, hg * hp + hh, mn, c_scr[hh], n_scr[hh], m_scr[hh])
            c_scr[hh] = c_new
            n_scr[hh] = n_new
            m_scr[hh] = m_new
            o_ref[sl, vs] = out.astype(o_ref.dtype)
        return carry

    lax.fori_loop(0, nchunks, body, 0)

    @pl.when(tb == nblocks - 1)
    def _():
        cf_ref[...] = c_scr[...]
        nf_ref[...] = n_scr[...]
        mf_ref[...] = m_scr[...]


def _mlstm(zb, gact, mnorm, c0, n0, m0, bsz, t):
    chunk = min(t, 64)
    tg = _tile(t, 256, chunk)
    nblocks = t // tg
    hp = SCAN_HEADS_PER_STEP
    ng = B_HEADS // hp
    row = lambda b, h, j: b * nblocks + j
    st4 = lambda b, h, j: (b, h, 0, 0)
    return pl.pallas_call(
        functools.partial(_mlstm_kernel, chunk=chunk, nchunks=tg // chunk, nblocks=nblocks),
        grid=(bsz, ng, nblocks),
        in_specs=[pl.BlockSpec((tg, hp * B_DK), lambda b, h, j: (row(b, h, j), h)),
                  pl.BlockSpec((tg, hp * B_DK), lambda b, h, j: (row(b, h, j), ng + h)),
                  pl.BlockSpec((tg, hp * B_DV), lambda b, h, j: (row(b, h, j), ng + h)),
                  pl.BlockSpec((tg, hp * B_DV), lambda b, h, j: (row(b, h, j), 2 * ng + h)),
                  pl.BlockSpec((tg, LANES), lambda b, h, j: (row(b, h, j), 0)),
                  pl.BlockSpec((1, B_DV), lambda b, h, j: (0, 0)),
                  pl.BlockSpec((None, hp, B_DK, B_DV), st4),
                  pl.BlockSpec((None, hp, 1, B_DK), st4),
                  pl.BlockSpec((None, hp, 1, LANES), st4)],
        out_specs=[pl.BlockSpec((tg, hp * B_DV), lambda b, h, j: (row(b, h, j), h)),
                   pl.BlockSpec((None, hp, B_DK, B_DV), st4),
                   pl.BlockSpec((None, hp, 1, B_DK), st4),
                   pl.BlockSpec((None, hp, 1, LANES), st4)],
        out_shape=[jax.ShapeDtypeStruct((bsz * t, B_WIDTH), _act_dtype(tg)),
                   jax.ShapeDtypeStruct((bsz, B_HEADS, B_DK, B_DV), F32),
                   jax.ShapeDtypeStruct((bsz, B_HEADS, 1, B_DK), F32),
                   jax.ShapeDtypeStruct((bsz, B_HEADS, 1, LANES), F32)],
        scratch_shapes=[pltpu.VMEM((hp, B_DK, B_DV), F32), pltpu.VMEM((hp, 1, B_DK), F32),
                        pltpu.VMEM((hp, 1, LANES), F32)],
        compiler_params=_cp("arbitrary", "arbitrary", "arbitrary"),
        name="mlstm",
    )(zb, zb, zb, zb, gact, mnorm.reshape(1, B_DV), c0,
      n0.reshape(bsz, B_HEADS, 1, B_DK),
      jnp.broadcast_to(m0[:, :, None, None], (bsz, B_HEADS, 1, LANES)))


FOX_HEADS_PER_STEP = 2


def _fox_prompt_kernel(q_ref, k_ref, v_ref, frow_ref, fcol_ref, o_ref, *, tq, tk):
    hp = FOX_HEADS_PER_STEP
    hg = pl.program_id(1)
    qi = pl.program_id(2)
    fc = fcol_ref[...]
    heads = []
    for hh in range(hp):
        head = hg * hp + hh
        heads.append((head, q_ref[:, hh * C_DH:(hh + 1) * C_DH], _lane_col(fc, head)))
    qpos = qi * tq + lax.broadcasted_iota(jnp.int32, (tq, tk), 0)
    kiota = lax.broadcasted_iota(jnp.int32, (tq, tk), 1)

    def step(j, carry, masked):
        k0 = pl.multiple_of(j * tk, tk)
        new = []
        for hh, (head, q, fcol) in enumerate(heads):
            m, l, acc = carry[hh]
            kj = k_ref[pl.ds(k0, tk), hh * C_DH:(hh + 1) * C_DH]
            vj = v_ref[pl.ds(k0, tk), hh * C_DH:(hh + 1) * C_DH]
            fr = frow_ref[pl.ds(head, 1), pl.ds(k0, tk)]
            s = _dot_nt(q, kj) * C_SCALE + (fcol - fr)
            if masked:
                s = jnp.where(k0 + kiota <= qpos, s, NEG_BIG)
            m_new = jnp.maximum(m, jnp.max(s, axis=-1, keepdims=True))
            alpha = jnp.exp(m - m_new)
            p = jnp.exp(s - m_new)
            l = alpha * l + jnp.sum(p, axis=-1, keepdims=True)
            acc = alpha * acc + _dot(p.astype(BF16), vj)
            new.append((m_new, l, acc))
        return tuple(new)

    one = (jnp.full((tq, 1), NEG_BIG, F32), jnp.zeros((tq, 1), F32), jnp.zeros((tq, C_DH), F32))
    ndiag = tq // tk
    nfull = qi * ndiag
    carry = lax.fori_loop(0, nfull, lambda j, c: step(j, c, False), (one,) * hp)
    for dblk in range(ndiag):
        carry = step(nfull + dblk, carry, True)
    for hh in range(hp):
        m, l, acc = carry[hh]
        o_ref[:, hh * C_DH:(hh + 1) * C_DH] = (acc / l).astype(o_ref.dtype)


def _fox_prompt(cq, ckb, cvb, frow, fcum, bsz, t):
    tq = _tile(t, 512, LANES)
    tk = _tile(tq, 256, LANES)
    nq = t // tq
    hp = FOX_HEADS_PER_STEP
    w = hp * C_DH
    return pl.pallas_call(
        functools.partial(_fox_prompt_kernel, tq=tq, tk=tk),
        grid=(bsz, C_HEADS // hp, nq),
        in_specs=[pl.BlockSpec((tq, w), lambda b, h, i: (b * nq + i, h)),
                  pl.BlockSpec((t, w), lambda b, h, i: (b, h)),
                  pl.BlockSpec((t, w), lambda b, h, i: (b, h)),
                  pl.BlockSpec((None, C_HEADS, t), lambda b, h, i: (b, 0, 0)),
                  pl.BlockSpec((tq, LANES), lambda b, h, i: (b * nq + i, 0))],
        out_specs=pl.BlockSpec((tq, w), lambda b, h, i: (b * nq + i, h)),
        out_shape=jax.ShapeDtypeStruct((bsz * t, C_WIDTH), _act_dtype(tq)),
        compiler_params=_cp("arbitrary", "arbitrary", "arbitrary"),
        name="fox_prompt",
    )(cq, ckb, cvb, frow, fcum)


def _page_suffix_kernel(lf_ref, rin_ref, tot_ref):
    pb = lf_ref.shape[0]
    x = lf_ref[...].reshape(pb * C_HEADS, PAGE)
    r_i = lax.broadcasted_iota(jnp.int32, (PAGE, PAGE), 0)
    c_i = lax.broadcasted_iota(jnp.int32, (PAGE, PAGE), 1)
    later = (r_i > c_i).astype(BF16)
    rin_ref[...] = _dot01_right(x, later).reshape(pb, C_HEADS, PAGE)
    tot = jnp.sum(x, axis=-1, keepdims=True)
    tot_ref[...] = jnp.broadcast_to(tot, (pb * C_HEADS, PAGE)).reshape(pb, C_HEADS, PAGE)


def _page_suffix(lft):
    n = lft.shape[0]
    pb = _tile(n, 64, 1)
    spec = pl.BlockSpec((pb, C_HEADS, PAGE), lambda i: (i, 0, 0))
    return pl.pallas_call(
        _page_suffix_kernel,
        grid=(n // pb,),
        in_specs=[spec],
        out_specs=[spec, spec],
        out_shape=[jax.ShapeDtypeStruct(lft.shape, F32)] * 2,
        compiler_params=_cp("arbitrary"),
        name="page_suffix",
    )(lft)


def _fox_sample_kernel(pt_ref, q_ref, gq_ref, gk_ref, kn_ref, vn_ref, *rest, pp, nsteps, tq):
    k_refs = rest[:pp]
    v_refs = rest[pp:2 * pp]
    rin_refs = rest[2 * pp:3 * pp]
    tot_refs = rest[3 * pp:4 * pp]
    o_ref = rest[4 * pp]
    m_scr, l_scr, acc_scr, carry_scr, madd_scr = rest[4 * pp + 1:]
    j = pl.program_id(1)
    rows = C_HEADS * tq
    cols = PAGE * C_HEADS

    @pl.when(j == 0)
    def _():
        m_scr[...] = jnp.full_like(m_scr, NEG_BIG)
        l_scr[...] = jnp.zeros_like(l_scr)
        acc_scr[...] = jnp.zeros_like(acc_scr)
        carry_scr[...] = jnp.zeros_like(carry_scr)
        r_i = lax.broadcasted_iota(jnp.int32, (rows, cols), 0)
        c_i = lax.broadcasted_iota(jnp.int32, (rows, cols), 1)
        madd_scr[...] = jnp.where(c_i % C_HEADS == r_i // tq, 0.0, NEG_BIG) + gq_ref[...]

    q = q_ref[...]

    def scores(k_blk, row_bias):
        k2 = k_blk.reshape(cols, C_DH).astype(BF16)
        return _dot_nt(q, k2) * C_SCALE + madd_scr[...] + row_bias

    def update(s_list, v_list):
        m_old = m_scr[...]
        m_new = m_old
        for s in s_list:
            m_new = jnp.maximum(m_new, jnp.max(s, axis=-1, keepdims=True))
        alpha = jnp.exp(m_old - m_new)
        l_new = alpha * l_scr[...]
        pv = None
        for s, v in zip(s_list, v_list):
            p = jnp.exp(s - m_new)
            l_new = l_new + jnp.sum(p, axis=-1, keepdims=True)
            d = _dot(p.astype(BF16), v.reshape(cols, C_DH).astype(BF16))
            pv = d if pv is None else pv + d
        m_scr[...] = m_new
        l_scr[...] = l_new
        acc_scr[...] = alpha * acc_scr[...] + pv

    carry = carry_scr[...]
    s_list = []
    for u in range(pp):
        s_list.append(scores(k_refs[u][...], rin_refs[u][...] + carry))
        carry = carry + tot_refs[u][...]
    carry_scr[...] = carry
    update(s_list, [v_refs[u][...] for u in range(pp)])

    @pl.when(j == nsteps - 1)
    def _():
        r_i = lax.broadcasted_iota(jnp.int32, (rows, cols), 0)
        c_i = lax.broadcasted_iota(jnp.int32, (rows, cols), 1)
        causal = jnp.where(c_i // C_HEADS <= r_i % tq, 0.0, NEG_BIG)
        update([scores(kn_ref[...], -gk_ref[...]) + causal], [vn_ref[...]])
        o_ref[...] = acc_scr[...] / l_scr[...]


def _fox_sample(cq, ck, cv, fcum, cache_k, cache_v, rin, tot, page_table, l, bsz, t):
    n_pages = page_table.shape[1]
    pp = 4 if n_pages % 4 == 0 else 1
    nsteps = n_pages // pp
    rows = C_HEADS * t
    cols = PAGE * C_HEADS
    q2 = cq.reshape(bsz, t, C_HEADS, C_DH).transpose(0, 2, 1, 3).reshape(bsz, rows, C_DH)
    g = fcum[:, :C_HEADS].reshape(bsz, t, C_HEADS)
    gq = g.transpose(0, 2, 1).reshape(bsz, rows, 1)
    gk = jnp.pad(g.reshape(bsz, 1, t * C_HEADS), ((0, 0), (0, 0), (0, cols - t * C_HEADS)))
    tok_pad = ((0, 0), (0, PAGE - t), (0, 0), (0, 0))
    kn = jnp.pad(ck.reshape(bsz, t, C_HEADS, C_DH), tok_pad)
    vn = jnp.pad(cv.reshape(bsz, t, C_HEADS, C_DH), tok_pad)

    def page_idx(b, j, pt, u):
        return pt[b * n_pages + n_pages - 1 - (j * pp + u)]

    def kv_spec(u):
        return pl.BlockSpec((None, None, PAGE, C_HEADS, C_DH),
                            lambda b, j, pt: (l, page_idx(b, j, pt, u), 0, 0, 0))

    def vec_spec(u):
        return pl.BlockSpec((None, None, 1, cols), lambda b, j, pt: (l, page_idx(b, j, pt, u), 0, 0))

    batch3 = lambda b, j, pt: (b, 0, 0)
    batch4 = lambda b, j, pt: (b, 0, 0, 0)
    in_specs = [pl.BlockSpec((None, rows, C_DH), batch3),
                pl.BlockSpec((None, rows, 1), batch3),
                pl.BlockSpec((None, 1, cols), batch3),
                pl.BlockSpec((None, PAGE, C_HEADS, C_DH), batch4),
                pl.BlockSpec((None, PAGE, C_HEADS, C_DH), batch4)]
    in_specs += [kv_spec(u) for u in range(pp)]
    in_specs += [kv_spec(u) for u in range(pp)]
    in_specs += [vec_spec(u) for u in range(pp)]
    in_specs += [vec_spec(u) for u in range(pp)]
    grid_spec = pltpu.PrefetchScalarGridSpec(
        num_scalar_prefetch=1,
        grid=(bsz, nsteps),
        in_specs=in_specs,
        out_specs=pl.BlockSpec((None, rows, C_DH), batch3),
        scratch_shapes=[pltpu.VMEM((rows, 1), F32), pltpu.VMEM((rows, 1), F32),
                        pltpu.VMEM((rows, C_DH), F32), pltpu.VMEM((1, cols), F32),
                        pltpu.VMEM((rows, cols), F32)],
    )
    out = pl.pallas_call(
        functools.partial(_fox_sample_kernel, pp=pp, nsteps=nsteps, tq=t),
        grid_spec=grid_spec,
        out_shape=jax.ShapeDtypeStruct((bsz, rows, C_DH), F32),
        compiler_params=_cp("arbitrary", "arbitrary"),
        name="fox_sample",
    )(page_table.reshape(-1), q2, gq, gk, kn, vn,
      *([cache_k] * pp), *([cache_v] * pp), *([rin] * pp), *([tot] * pp))
    return out.reshape(bsz, C_HEADS, t, C_DH).transpose(0, 2, 1, 3).reshape(bsz * t, C_WIDTH)


def _merge_kernel(h_ref, oa_ref, ob_ref, oc_ref, ga_ref, gb_ref, gc_ref, wa_ref, wb_ref, wc_ref, o_ref,
                  wab, wbb, wcb):
    @pl.when(pl.program_id(1) == 0)
    def _():
        wab[...] = wa_ref[...].astype(BF16)
        wbb[...] = wb_ref[...].astype(BF16)
        wcb[...] = wc_ref[...].astype(BF16)

    h = h_ref[...]
    out = _sigmoid(_dot(h, ga_ref[...])) * _dot(oa_ref[...].astype(BF16), wab[...])
    out = out + _sigmoid(_dot(h, gb_ref[...])) * _dot(ob_ref[...].astype(BF16), wbb[...])
    out = out + _sigmoid(_dot(h, gc_ref[...])) * _dot(oc_ref[...].astype(BF16), wcb[...])
    o_ref[...] = out.astype(o_ref.dtype)


def _merge(h, oa, ob, oc, wg, wpa, wpb, wpc, l):
    m, d = h.shape
    tm = _tile(m, 512, 16)
    tn = _tile(d, 256, LANES)
    nb = d // tn
    act = lambda w: pl.BlockSpec((tm, w), lambda j, i: (i, 0))
    wgt = lambda w: pl.BlockSpec((None, w, tn), lambda j, i: (l, 0, j))
    gate = lambda g: pl.BlockSpec((None, d, tn), lambda j, i: (l, 0, g * nb + j))
    return pl.pallas_call(
        _merge_kernel,
        grid=(nb, m // tm),
        in_specs=[act(d), act(A_WIDTH), act(B_WIDTH), act(C_WIDTH), gate(0), gate(1), gate(2),
                  wgt(A_WIDTH), wgt(B_WIDTH), wgt(C_WIDTH)],
        out_specs=pl.BlockSpec((tm, tn), lambda j, i: (i, j)),
        out_shape=jax.ShapeDtypeStruct((m, d), BF16),
        scratch_shapes=[pltpu.VMEM((A_WIDTH, tn), BF16), pltpu.VMEM((B_WIDTH, tn), BF16),
                        pltpu.VMEM((C_WIDTH, tn), BF16)],
        compiler_params=_cp("arbitrary", "arbitrary"),
        name="gated_merge",
    )(h, oa, ob, oc, wg, wg, wg, wpa, wpb, wpc)


def _prep_w_in(w_in):
    depth, d, _ = w_in.shape
    w_c = w_in[:, :, _O_C:_O_CF].astype(BF16)
    w_g = w_in[:, :, _O_G:].astype(BF16)
    pad = jnp.zeros((depth, d, LANES - C_HEADS - 2 * B_HEADS), BF16)
    w_s = jnp.concatenate([w_in[:, :, _O_CF:_O_G].astype(BF16), w_in[:, :, _O_BI:_O_C].astype(BF16), pad],
                          axis=-1)
    return w_c, w_g, w_s


def _layer(x, l, w, bsz, t, states, past):
    (ffn1_norm, ffn1_w1, ffn1_w3, ffn1_w2, mix_norm, w_in, w_c, w_g, w_s, lb_all, hgrn_norm, gate_bias,
     mlstm_norm, w_proj_a, w_proj_b, w_proj_c, w_out, ffn2_norm, ffn2_w1, ffn2_w3, ffn2_w2) = w
    s_a0, c0, n0, m0 = states

    h = _rmsnorm(x, ffn1_norm[l], BF16)
    x = _down(_ffn_up(h, ffn1_w1, ffn1_w3, l), ffn1_w2, x, l, 0.5)

    h = _rmsnorm(x, mix_norm[l], BF16)
    za = _proj(h, w_in, l, 0, 4 * A_WIDTH, F32)
    zb = _proj(h, w_in, l, _O_B, ZB_W, F32)
    zs = _proj(h, w_s, l, 0, LANES, F32)
    cq = _proj(h, w_c, l, 0, C_WIDTH, BF16)
    ck = _proj(h, w_c, l, C_WIDTH, C_WIDTH, F32)
    cv = _proj(h, w_c, l, 2 * C_WIDTH, C_WIDTH, F32)

    oa, s_a = _hgrn(za, lb_all[l], hgrn_norm[l], s_a0, bsz, t)
    if past is None:
        gact, fcum, frow = _gates(zs, gate_bias[l], bsz, t, True)
    else:
        gact, fcum = _gates(zs, gate_bias[l], bsz, t, False)
    ob, c_b, n_b, m_b = _mlstm(zb, gact, mlstm_norm[l], c0, n0, m0, bsz, t)
    if past is None:
        oc = _fox_prompt(cq, ck.astype(BF16), cv.astype(BF16), frow, fcum, bsz, t)
    else:
        oc = _fox_sample(cq, ck, cv, fcum, *past, l, bsz, t)

    merged = _merge(h, oa, ob, oc, w_g, w_proj_a, w_proj_b, w_proj_c, l)
    x = _down(merged, w_out, x, l, 1.0)

    h = _rmsnorm(x, ffn2_norm[l], BF16)
    x = _down(_ffn_up(h, ffn2_w1, ffn2_w3, l), ffn2_w2, x, l, 0.5)

    st = (ck.reshape(bsz, t, C_HEADS, C_DH), cv.reshape(bsz, t, C_HEADS, C_DH),
          gact[:, :C_HEADS].reshape(bsz, t, C_HEADS), s_a, c_b,
          n_b.reshape(bsz, B_HEADS, B_DK), m_b[:, :, 0, 0])
    return x, st


def kernel(x_prompt, x_sample, cache_k, cache_v, cache_logf, state_hgrn, state_mlstm_c, state_mlstm_n,
           state_mlstm_m, page_table, ffn1_norm, ffn1_w1, ffn1_w3, ffn1_w2, mix_norm, w_in, hgrn_lb,
           hgrn_norm, mlstm_b_i, mlstm_b_f, mlstm_norm, fox_b_f, w_proj_a, w_proj_b, w_proj_c, w_out,
           ffn2_norm, ffn2_w1, ffn2_w3, ffn2_w2, final_norm):
    depth = w_in.shape[0]
    bp, tp, d = x_prompt.shape
    bs, ts, _ = x_sample.shape
    n_pool = cache_k.shape[1]

    lb_all = _lower_bounds(hgrn_lb)
    w_c, w_g, w_s = _prep_w_in(w_in)
    gate_bias = jnp.concatenate(
        [fox_b_f, mlstm_b_i, mlstm_b_f, jnp.zeros((depth, LANES - C_HEADS - 2 * B_HEADS), F32)],
        axis=-1).reshape(depth, 1, LANES).astype(F32)
    w = (ffn1_norm, ffn1_w1, ffn1_w3, ffn1_w2.astype(BF16), mix_norm, w_in, w_c, w_g, w_s, lb_all, hgrn_norm,
         gate_bias, mlstm_norm, w_proj_a, w_proj_b, w_proj_c, w_out, ffn2_norm, ffn2_w1, ffn2_w3,
         ffn2_w2.astype(BF16))

    lft = jnp.swapaxes(cache_logf.astype(F32), -1, -2).reshape(depth * n_pool, C_HEADS, PAGE)
    flat = lambda a: jnp.swapaxes(a, -1, -2).reshape(depth, n_pool, 1, PAGE * C_HEADS)
    rin, tot = (flat(a) for a in _page_suffix(lft))

    zero_states = (jnp.zeros((bp, A_HEADS, A_DK, A_DV), F32), jnp.zeros((bp, B_HEADS, B_DK, B_DV), F32),
                   jnp.zeros((bp, B_HEADS, B_DK), F32), jnp.zeros((bp, B_HEADS), F32))

    yp = x_prompt.reshape(bp * tp, d)
    ys = x_sample.reshape(bs * ts, d)
    p_states = [[] for _ in range(7)]
    s_states = [[] for _ in range(7)]
    for l in range(depth):
        yp, st_p = _layer(yp, l, w, bp, tp, zero_states, None)
        ys, st_s = _layer(ys, l, w, bs, ts,
                          (state_hgrn[l], state_mlstm_c[l], state_mlstm_n[l], state_mlstm_m[l]),
                          (cache_k, cache_v, rin, tot, page_table))
        for i in range(7):
            p_states[i].append(st_p[i])
            s_states[i].append(st_s[i])

    yp = _rmsnorm(yp, final_norm, F32).reshape(bp, tp, d)
    ys = _rmsnorm(ys, final_norm, F32).reshape(bs, ts, d)
    outs_p = [jnp.stack(s, axis=0) for s in p_states]
    outs_s = [jnp.stack(s, axis=0) for s in s_states]
    return (yp, ys, *outs_p, *outs_s)
```

```python
import functools

import jax
import jax.numpy as jnp
from jax import lax
from jax.experimental import pallas as pl
from jax.experimental.pallas import tpu as pltpu

F32 = jnp.float32
BF16 = jnp.bfloat16

A_HEADS, A_DK, A_DV = 8, 128, 128
B_HEADS, B_DK, B_DV = 4, 128, 256
C_HEADS, C_DH = 16, 128
A_WIDTH = A_HEADS * A_DV
B_WIDTH = B_HEADS * B_DV
C_WIDTH = C_HEADS * C_DH
GATE_SOFTCAP = 15.0
C_SCALE = C_DH ** -0.5
EPS = 1e-6
NEG_BIG = -1e30
TINY = 1e-30
PAGE = 128

LANES = 128
VMEM_LIMIT = 56 * 1024 * 1024

_O_B = 4 * A_WIDTH
_O_BI = _O_B + 2 * B_HEADS * B_DK + 2 * B_WIDTH
_O_BF = _O_BI + B_HEADS
_O_C = _O_BF + B_HEADS
_O_CF = _O_C + 3 * C_WIDTH
_O_G = _O_CF + C_HEADS

ZB_W = 2 * B_HEADS * B_DK + 2 * B_WIDTH
LANE_I = C_HEADS
LANE_F = C_HEADS + B_HEADS
SCAN_HEADS_PER_STEP = 2


def _cp(*sem):
    return pltpu.CompilerParams(dimension_semantics=sem, vmem_limit_bytes=VMEM_LIMIT)


def _tile(n, pref, mult=8):
    if n <= pref:
        return n
    t = (pref // mult) * mult
    while t >= mult:
        if n % t == 0:
            return t
        t -= mult
    return n


ROW_TILE = 1376
ACC_BYTES = 16 * 1024 * 1024


def _row_tile(m, cap=ROW_TILE):
    return _tile(m, cap, 16)


def _act_dtype(rows):
    return BF16 if rows % 16 == 0 else F32


def _sigmoid(x):
    return jax.nn.sigmoid(x)


def _log_sigmoid(x):
    return jnp.minimum(x, 0.0) - jnp.log(1.0 + jnp.exp(-jnp.abs(x)))


def _dot(a, b):
    return jnp.dot(a, b, preferred_element_type=F32)


def _dot_nt(a, b):
    return lax.dot_general(a, b, (((1,), (1,)), ((), ())), preferred_element_type=F32)


def _dot_tn(a, b):
    return lax.dot_general(a, b, (((0,), (0,)), ((), ())), preferred_element_type=F32)


def _split3(x):
    hi = x.astype(BF16)
    r1 = x - hi.astype(F32)
    mid = r1.astype(BF16)
    lo = (r1 - mid.astype(F32)).astype(BF16)
    return hi, mid, lo


def _dot01_left(m01, x):
    hi, mid, lo = _split3(x)
    return (_dot(m01, hi) + _dot(m01, mid)) + _dot(m01, lo)


def _dot01_right(x, m01):
    hi, mid, lo = _split3(x)
    return (_dot(hi, m01) + _dot(mid, m01)) + _dot(lo, m01)


def _tri_incl(n):
    r = lax.broadcasted_iota(jnp.int32, (n, n), 0)
    c = lax.broadcasted_iota(jnp.int32, (n, n), 1)
    return r >= c


def _cumsum_rows(x):
    n = x.shape[0]
    return _dot01_left(_tri_incl(n).astype(BF16), x)


def _lane_col(x, lane):
    idx = lax.broadcasted_iota(jnp.int32, x.shape, 1)
    return jnp.sum(jnp.where(idx == lane, x, 0.0), axis=-1, keepdims=True)


def _col_to_row(col):
    n = col.shape[0]
    r = lax.broadcasted_iota(jnp.int32, (n, n), 0)
    c = lax.broadcasted_iota(jnp.int32, (n, n), 1)
    return jnp.sum(jnp.where(r == c, jnp.broadcast_to(col, (n, n)), 0.0), axis=0, keepdims=True)


def _lb_kernel(x_ref, o_ref):
    x = x_ref[...]
    e = jnp.exp(x - jnp.max(x, axis=0, keepdims=True))
    w = e / jnp.sum(e, axis=0, keepdims=True)
    run = jnp.zeros_like(w[0:1])
    for l in range(x.shape[0]):
        o_ref[l:l + 1, :] = jnp.clip(run, 0.0, 1.0)
        run = run + w[l:l + 1]


def _lower_bounds(hgrn_lb):
    return pl.pallas_call(
        _lb_kernel, out_shape=jax.ShapeDtypeStruct(hgrn_lb.shape, F32), name="hgrn_lower_bounds",
    )(hgrn_lb.astype(F32))


def _rmsnorm_kernel(x_ref, g_ref, o_ref):
    x = x_ref[...]
    y = x * lax.rsqrt(jnp.mean(x * x, axis=-1, keepdims=True) + EPS)
    o_ref[...] = (y * g_ref[...]).astype(o_ref.dtype)


def _rmsnorm(x, g, out_dtype):
    m, d = x.shape
    tm = _tile(m, 256)
    return pl.pallas_call(
        _rmsnorm_kernel,
        grid=(m // tm,),
        in_specs=[pl.BlockSpec((tm, d), lambda i: (i, 0)), pl.BlockSpec((1, d), lambda i: (0, 0))],
        out_specs=pl.BlockSpec((tm, d), lambda i: (i, 0)),
        out_shape=jax.ShapeDtypeStruct((m, d), out_dtype),
        compiler_params=_cp("arbitrary"),
        name="rmsnorm",
    )(x, g.reshape(1, d))


def _ffn_up_kernel(h_ref, w1_ref, w3_ref, o_ref, w1b, w3b):
    @pl.when(pl.program_id(1) == 0)
    def _():
        w1b[...] = w1_ref[...].astype(BF16)
        w3b[...] = w3_ref[...].astype(BF16)

    h = h_ref[...]
    a = _dot(h, w1b[...])
    b = _dot(h, w3b[...])
    o_ref[...] = (a * _sigmoid(a) * b).astype(o_ref.dtype)


def _ffn_up(h, w1, w3, l):
    m, d = h.shape
    f = w1.shape[-1]
    tm = _row_tile(m)
    tn = _tile(f, 256, LANES)
    return pl.pallas_call(
        _ffn_up_kernel,
        grid=(f // tn, m // tm),
        in_specs=[pl.BlockSpec((tm, d), lambda j, i: (i, 0)),
                  pl.BlockSpec((None, d, tn), lambda j, i: (l, 0, j)),
                  pl.BlockSpec((None, d, tn), lambda j, i: (l, 0, j))],
        out_specs=pl.BlockSpec((tm, tn), lambda j, i: (i, j)),
        out_shape=jax.ShapeDtypeStruct((m, f), BF16),
        scratch_shapes=[pltpu.VMEM((d, tn), BF16), pltpu.VMEM((d, tn), BF16)],
        compiler_params=_cp("arbitrary", "arbitrary"),
        name="ffn_up",
    )(h, w1, w3)


def _down_kernel(a_ref, w_ref, r_ref, o_ref, *acc, scale, nk):
    k = pl.program_id(1)
    j = pl.program_id(2)
    p = _dot(a_ref[...], w_ref[...].astype(BF16))
    if nk == 1:
        o_ref[...] = r_ref[...] + scale * p
        return
    acc_ref = acc[0]

    @pl.when(k == 0)
    def _():
        acc_ref[j] = p

    @pl.when((k > 0) & (k < nk - 1))
    def _():
        acc_ref[j] += p

    @pl.when(k == nk - 1)
    def _():
        o_ref[...] = r_ref[...] + scale * (acc_ref[j] + p)


def _down(a, w, res, l, scale):
    m, kdim = a.shape
    n = w.shape[-1]
    tn = _tile(n, 256, LANES)
    tk = kdim if kdim <= 4096 else kdim // 2
    assert kdim % tk == 0 and tk % LANES == 0
    nk = kdim // tk
    tm = _row_tile(m) if nk == 1 else _row_tile(m, min(ROW_TILE, ACC_BYTES // (4 * n)))
    nj = n // tn
    out_idx = lambda i, k, j: (i, jnp.where(k == nk - 1, j, 0))
    return pl.pallas_call(
        functools.partial(_down_kernel, scale=scale, nk=nk),
        grid=(m // tm, nk, nj),
        in_specs=[pl.BlockSpec((tm, tk), lambda i, k, j: (i, k)),
                  pl.BlockSpec((None, tk, tn), lambda i, k, j: (l, k, j)),
                  pl.BlockSpec((tm, tn), out_idx)],
        out_specs=pl.BlockSpec((tm, tn), out_idx),
        out_shape=jax.ShapeDtypeStruct((m, n), F32),
        scratch_shapes=[pltpu.VMEM((nj, tm, tn), F32)] if nk > 1 else [],
        compiler_params=_cp("arbitrary", "arbitrary", "arbitrary"),
        name="down_residual",
    )(a, w, res)


def _proj_kernel(h_ref, w_ref, o_ref, *wb):
    if wb:
        @pl.when(pl.program_id(1) == 0)
        def _():
            wb[0][...] = w_ref[...].astype(BF16)
        w = wb[0][...]
    else:
        w = w_ref[...]
    o_ref[...] = _dot_nt(h_ref[...], w).astype(o_ref.dtype)


def _proj(h, w_t, l, row_off, nrows, out_dtype):
    m, d = h.shape
    tm = _row_tile(m)
    tn = _tile(nrows, 512, LANES)
    assert row_off % tn == 0 and nrows % tn == 0
    rb = row_off // tn
    scratch = [pltpu.VMEM((tn, d), BF16)] if w_t.dtype != BF16 else []
    return pl.pallas_call(
        _proj_kernel,
        grid=(nrows // tn, m // tm),
        in_specs=[pl.BlockSpec((tm, d), lambda j, i: (i, 0)),
                  pl.BlockSpec((None, tn, d), lambda j, i: (l, rb + j, 0))],
        out_specs=pl.BlockSpec((tm, tn), lambda j, i: (i, j)),
        out_shape=jax.ShapeDtypeStruct((m, nrows), out_dtype),
        scratch_shapes=scratch,
        compiler_params=_cp("arbitrary", "arbitrary"),
        name="in_proj",
    )(h, w_t)


def _gates_kernel(s_ref, b_ref, act_ref, cum_ref, *rest, emit_rows):
    if emit_rows:
        row_ref, carry = rest
    else:
        (carry,) = rest

    @pl.when(pl.program_id(1) == 0)
    def _():
        carry[...] = jnp.zeros_like(carry)

    x = s_ref[...] + b_ref[...]
    lane = lax.broadcasted_iota(jnp.int32, x.shape, 1)
    capped = GATE_SOFTCAP * jnp.tanh(x / GATE_SOFTCAP)
    ls = _log_sigmoid(jnp.where(lane < LANE_I, x, capped))
    act = jnp.where((lane >= LANE_I) & (lane < LANE_F), capped, ls)
    act = jnp.where(lane < LANE_F + B_HEADS, act, 0.0)
    act_ref[...] = act
    c = _cumsum_rows(act) + carry[...]
    cum_ref[...] = c
    carry[...] = c[c.shape[0] - 1:, :]
    if emit_rows:
        row_ref[...] = c.T


def _gates(zs, bias, bsz, t, row0, emit_rows):
    m = bsz * t
    tb = _tile(t, LANES)
    nb = t // tb
    out_shape = [jax.ShapeDtypeStruct((m, LANES), F32), jax.ShapeDtypeStruct((m, LANES), F32)]
    out_specs = [pl.BlockSpec((tb, LANES), lambda b, j: (b * nb + j, 0)),
                 pl.BlockSpec((tb, LANES), lambda b, j: (b * nb + j, 0))]
    if emit_rows:
        out_shape.append(jax.ShapeDtypeStruct((bsz, LANES, t), F32))
        out_specs.append(pl.BlockSpec((None, LANES, tb), lambda b, j: (b, 0, j)))
    return pl.pallas_call(
        functools.partial(_gates_kernel, emit_rows=emit_rows),
        grid=(bsz, nb),
        in_specs=[pl.BlockSpec((tb, LANES), lambda b, j: (row0 // tb + b * nb + j, 0)),
                  pl.BlockSpec((1, LANES), lambda b, j: (0, 0))],
        out_specs=out_specs,
        out_shape=out_shape,
        scratch_shapes=[pltpu.VMEM((1, LANES), F32)],
        compiler_params=_cp("arbitrary", "arbitrary"),
        name="gate_activations",
    )(zs, bias)


def _hgrn_chunk(aq, af, ai, ag, lb, hn, s_prev, sub):
    c_len = aq.shape[0]
    q = aq * _sigmoid(aq)
    f = lb + (1.0 - lb) * _sigmoid(af)
    lf = jnp.log(jnp.maximum(f, TINY))
    k = 1.0 - f
    v = ai
    g = _cumsum_rows(lf)
    o = _dot((q * jnp.exp(g)).astype(BF16), s_prev.astype(BF16))
    rows = lax.broadcasted_iota(jnp.int32, (sub, A_DK), 0)
    parts = []
    for i0 in range(0, c_len, sub):
        g_i = g[i0:i0 + sub]
        q_i = q[i0:i0 + sub]
        o_i = o[i0:i0 + sub]
        if i0 > 0:
            r = g[i0 - 1:i0]
            q_t = (q_i * jnp.exp(g_i - r)).astype(BF16)
            k_t = (k[:i0] * jnp.exp(r - g[:i0])).astype(BF16)
            a = _dot_nt(q_t, k_t)
            o_i = o_i + _dot(a.astype(BF16), v[:i0].astype(BF16))
        for s in range(sub):
            e = jnp.exp(jnp.minimum(g_i - g_i[s:s + 1], 0.0))
            w = jnp.where(rows >= s, q_i * (k[i0 + s:i0 + s + 1] * e), 0.0)
            o_i = o_i + jnp.sum(w, axis=-1, keepdims=True) * v[i0 + s:i0 + s + 1]
        parts.append(o_i)
    o = parts[0] if len(parts) == 1 else jnp.concatenate(parts, axis=0)
    g_last = g[c_len - 1:c_len]
    k_dec = (k * jnp.exp(g_last - g)).astype(BF16)
    dec_col = _row_to_col(jnp.exp(g_last))
    s_new = dec_col * s_prev + _dot_tn(k_dec, v.astype(BF16))
    y = o * lax.rsqrt(jnp.mean(o * o, axis=-1, keepdims=True) + EPS) * hn
    return y * (ag * _sigmoid(ag)), s_new


def _row_to_col(row):
    n = row.shape[1]
    r = lax.broadcasted_iota(jnp.int32, (n, n), 0)
    c = lax.broadcasted_iota(jnp.int32, (n, n), 1)
    return jnp.sum(jnp.where(r == c, jnp.broadcast_to(row, (n, n)), 0.0), axis=1, keepdims=True)


def _hgrn_kernel(aq_ref, af_ref, ai_ref, ag_ref, lb_ref, hn_ref, s0_ref, o_ref, sfin_ref, s_scr,
                 *, chunk, sub, nchunks, nblocks):
    tb = pl.program_id(2)

    @pl.when(tb == 0)
    def _():
        s_scr[...] = s0_ref[...]

    hn = hn_ref[...]

    def body(ci, carry):
        r0 = pl.multiple_of(ci * chunk, chunk)
        sl = pl.ds(r0, chunk)
        for hh in range(SCAN_HEADS_PER_STEP):
            cs = slice(hh * A_DK, (hh + 1) * A_DK)
            out, s_new = _hgrn_chunk(aq_ref[sl, cs], af_ref[sl, cs], ai_ref[sl, cs], ag_ref[sl, cs],
                                     lb_ref[:, cs], hn, s_scr[hh], sub)
            s_scr[hh] = s_new
            o_ref[sl, cs] = out.astype(o_ref.dtype)
        return carry

    lax.fori_loop(0, nchunks, body, 0)

    @pl.when(tb == nblocks - 1)
    def _():
        sfin_ref[...] = s_scr[...]


def _hgrn(za, lb, hnorm, s0, bsz, t, row0):
    chunk = min(t, 64)
    sub = min(chunk, 16)
    tg = _tile(t, 256, chunk)
    nblocks = t // tg
    hp = SCAN_HEADS_PER_STEP
    ng = A_HEADS // hp

    def seg(s):
        return pl.BlockSpec((tg, hp * A_DK), lambda b, h, j: (row0 // tg + b * nblocks + j, s * ng + h))

    state = pl.BlockSpec((None, hp, A_DK, A_DV), lambda b, h, j: (b, h, 0, 0))
    return pl.pallas_call(
        functools.partial(_hgrn_kernel, chunk=chunk, sub=sub, nchunks=tg // chunk, nblocks=nblocks),
        grid=(bsz, ng, nblocks),
        in_specs=[seg(0), seg(1), seg(2), seg(3),
                  pl.BlockSpec((1, hp * A_DK), lambda b, h, j: (0, h)),
                  pl.BlockSpec((1, A_DV), lambda b, h, j: (0, 0)),
                  state],
        out_specs=[pl.BlockSpec((tg, hp * A_DV), lambda b, h, j: (b * nblocks + j, h)), state],
        out_shape=[jax.ShapeDtypeStruct((bsz * t, A_WIDTH), _act_dtype(tg)),
                   jax.ShapeDtypeStruct((bsz, A_HEADS, A_DK, A_DV), F32)],
        scratch_shapes=[pltpu.VMEM((hp, A_DK, A_DV), F32)],
        compiler_params=_cp("arbitrary", "arbitrary", "arbitrary"),
        name="hgrn2",
    )(za, za, za, za, lb.reshape(1, A_WIDTH), hnorm.reshape(1, A_DV), s0)


def _mlstm_chunk(q, k, v, og, gact, head, mn, c_prev, n_prev, m_prev):
    c_len = q.shape[0]
    logi = _lane_col(gact, LANE_I + head)
    fcum = _lane_col(_cumsum_rows(gact), LANE_F + head)
    m0 = m_prev[:, 0:1]
    causal = _tri_incl(c_len)
    dmat = jnp.where(causal, fcum - _col_to_row(fcum) + _col_to_row(logi), NEG_BIG)
    b_inter = fcum + m0
    m_t = jnp.maximum(b_inter, jnp.max(dmat, axis=1, keepdims=True))
    w_inter = jnp.exp(b_inter - m_t)
    qb = q.astype(BF16)
    ks = k * (B_DK ** -0.5)
    pw = _dot_nt(qb, ks.astype(BF16)) * jnp.exp(dmat - m_t)
    num = w_inter * _dot(qb, c_prev.astype(BF16)) + _dot(pw.astype(BF16), v.astype(BF16))
    den = w_inter * jnp.sum(q * n_prev, axis=-1, keepdims=True) + jnp.sum(pw, axis=-1, keepdims=True)
    h_out = num / jnp.maximum(jnp.abs(den), jnp.exp(-m_t))
    f_last = fcum[c_len - 1:c_len]
    m_new = m_t[c_len - 1:c_len]
    decay = jnp.exp(f_last + m0 - m_new)
    w_st = jnp.exp(f_last - fcum + logi - m_new)
    kw = ks * w_st
    c_new = decay * c_prev + _dot_tn(kw.astype(BF16), v.astype(BF16))
    n_new = decay * n_prev + jnp.sum(kw, axis=0, keepdims=True)
    y = h_out * lax.rsqrt(jnp.mean(h_out * h_out, axis=-1, keepdims=True) + EPS) * mn
    return y * _sigmoid(og), c_new, n_new, jnp.broadcast_to(m_new, m_prev.shape)


def _mlstm_kernel(q_ref, k_ref, v_ref, og_ref, g_ref, mn_ref, c0_ref, n0_ref, m0_ref,
                  o_ref, cf_ref, nf_ref, mf_ref, c_scr, n_scr, m_scr, *, chunk, nchunks, nblocks):
    hp = SCAN_HEADS_PER_STEP
    hg = pl.program_id(1)
    tb = pl.program_id(2)

    @pl.when(tb == 0)
    def _():
        c_scr[...] = c0_ref[...]
        n_scr[...] = n0_ref[...]
        m_scr[...] = m0_ref[...]

    mn = mn_ref[...]

    def body(ci, carry):
        r0 = pl.multiple_of(ci * chunk, chunk)
        sl = pl.ds(r0, chunk)
        gact = g_ref[sl, :]
        for hh in range(hp):
            ks = slice(hh * B_DK, (hh + 1) * B_DK)
            vs = slice(hh * B_DV, (hh + 1) * B_DV)
            out, c_new, n_new, m_new = _mlstm_chunk(q_ref[sl, ks], k_ref[sl, ks], v_ref[sl, vs], og_ref[sl, vs],
                                                    gact, hg * hp + hh, mn, c_scr[hh], n_scr[hh], m_scr[hh])
            c_scr[hh] = c_new
            n_scr[hh] = n_new
            m_scr[hh] = m_new
            o_ref[sl, vs] = out.astype(o_ref.dtype)
        return carry

    lax.fori_loop(0, nchunks, body, 0)

    @pl.when(tb == nblocks - 1)
    def _():
        cf_ref[...] = c_scr[...]
        nf_ref[...] = n_scr[...]
        mf_ref[...] = m_scr[...]


def _mlstm(zb, gact, mnorm, c0, n0, m0, bsz, t, row0):
    chunk = min(t, 64)
    tg = _tile(t, 256, chunk)
    nblocks = t // tg
    hp = SCAN_HEADS_PER_STEP
    ng = B_HEADS // hp
    row = lambda b, h, j: b * nblocks + j
    zrow = lambda b, h, j: row0 // tg + row(b, h, j)
    st4 = lambda b, h, j: (b, h, 0, 0)
    return pl.pallas_call(
        functools.partial(_mlstm_kernel, chunk=chunk, nchunks=tg // chunk, nblocks=nblocks),
        grid=(bsz, ng, nblocks),
        in_specs=[pl.BlockSpec((tg, hp * B_DK), lambda b, h, j: (zrow(b, h, j), h)),
                  pl.BlockSpec((tg, hp * B_DK), lambda b, h, j: (zrow(b, h, j), ng + h)),
                  pl.BlockSpec((tg, hp * B_DV), lambda b, h, j: (zrow(b, h, j), ng + h)),
                  pl.BlockSpec((tg, hp * B_DV), lambda b, h, j: (zrow(b, h, j), 2 * ng + h)),
                  pl.BlockSpec((tg, LANES), lambda b, h, j: (row(b, h, j), 0)),
                  pl.BlockSpec((1, B_DV), lambda b, h, j: (0, 0)),
                  pl.BlockSpec((None, hp, B_DK, B_DV), st4),
                  pl.BlockSpec((None, hp, 1, B_DK), st4),
                  pl.BlockSpec((None, hp, 1, LANES), st4)],
        out_specs=[pl.BlockSpec((tg, hp * B_DV), lambda b, h, j: (row(b, h, j), h)),
                   pl.BlockSpec((None, hp, B_DK, B_DV), st4),
                   pl.BlockSpec((None, hp, 1, B_DK), st4),
                   pl.BlockSpec((None, hp, 1, LANES), st4)],
        out_shape=[jax.ShapeDtypeStruct((bsz * t, B_WIDTH), _act_dtype(tg)),
                   jax.ShapeDtypeStruct((bsz, B_HEADS, B_DK, B_DV), F32),
                   jax.ShapeDtypeStruct((bsz, B_HEADS, 1, B_DK), F32),
                   jax.ShapeDtypeStruct((bsz, B_HEADS, 1, LANES), F32)],
        scratch_shapes=[pltpu.VMEM((hp, B_DK, B_DV), F32), pltpu.VMEM((hp, 1, B_DK), F32),
                        pltpu.VMEM((hp, 1, LANES), F32)],
        compiler_params=_cp("arbitrary", "arbitrary", "arbitrary"),
        name="mlstm",
    )(zb, zb, zb, zb, gact, mnorm.reshape(1, B_DV), c0,
      n0.reshape(bsz, B_HEADS, 1, B_DK),
      jnp.broadcast_to(m0[:, :, None, None], (bsz, B_HEADS, 1, LANES)))


FOX_HEADS_PER_STEP = 2


def _fox_prompt_kernel(q_ref, k_ref, v_ref, frow_ref, fcol_ref, o_ref, *, tq, tk):
    hp = FOX_HEADS_PER_STEP
    hg = pl.program_id(1)
    qi = pl.program_id(2)
    fc = fcol_ref[...]
    heads = []
    for hh in range(hp):
        head = hg * hp + hh
        heads.append((head, q_ref[:, hh * C_DH:(hh + 1) * C_DH], _lane_col(fc, head)))
    qpos = qi * tq + lax.broadcasted_iota(jnp.int32, (tq, tk), 0)
    kiota = lax.broadcasted_iota(jnp.int32, (tq, tk), 1)

    def step(j, carry, masked):
        k0 = pl.multiple_of(j * tk, tk)
        new = []
        for hh, (head, q, fcol) in enumerate(heads):
            m, l, acc = carry[hh]
            kj = k_ref[pl.ds(k0, tk), hh * C_DH:(hh + 1) * C_DH]
            vj = v_ref[pl.ds(k0, tk), hh * C_DH:(hh + 1) * C_DH]
            fr = frow_ref[pl.ds(head, 1), pl.ds(k0, tk)]
            s = _dot_nt(q, kj) * C_SCALE + (fcol - fr)
            if masked:
                s = jnp.where(k0 + kiota <= qpos, s, NEG_BIG)
            m_new = jnp.maximum(m, jnp.max(s, axis=-1, keepdims=True))
            alpha = jnp.exp(m - m_new)
            p = jnp.exp(s - m_new)
            l = alpha * l + jnp.sum(p, axis=-1, keepdims=True)
            acc = alpha * acc + _dot(p.astype(BF16), vj)
            new.append((m_new, l, acc))
        return tuple(new)

    one = (jnp.full((tq, 1), NEG_BIG, F32), jnp.zeros((tq, 1), F32), jnp.zeros((tq, C_DH), F32))
    ndiag = tq // tk
    nfull = qi * ndiag
    carry = lax.fori_loop(0, nfull, lambda j, c: step(j, c, False), (one,) * hp)
    for dblk in range(ndiag):
        carry = step(nfull + dblk, carry, True)
    for hh in range(hp):
        m, l, acc = carry[hh]
        o_ref[:, hh * C_DH:(hh + 1) * C_DH] = (acc / l).astype(o_ref.dtype)


def _fox_prompt(cq, ckb, cvb, frow, fcum, bsz, t):
    tq = _tile(t, 512, LANES)
    tk = _tile(tq, 256, LANES)
    nq = t // tq
    hp = FOX_HEADS_PER_STEP
    w = hp * C_DH
    return pl.pallas_call(
        functools.partial(_fox_prompt_kernel, tq=tq, tk=tk),
        grid=(bsz, C_HEADS // hp, nq),
        in_specs=[pl.BlockSpec((tq, w), lambda b, h, i: (b * nq + i, h)),
                  pl.BlockSpec((t, w), lambda b, h, i: (b, h)),
                  pl.BlockSpec((t, w), lambda b, h, i: (b, h)),
                  pl.BlockSpec((None, C_HEADS, t), lambda b, h, i: (b, 0, 0)),
                  pl.BlockSpec((tq, LANES), lambda b, h, i: (b * nq + i, 0))],
        out_specs=pl.BlockSpec((tq, w), lambda b, h, i: (b * nq + i, h)),
        out_shape=jax.ShapeDtypeStruct((bsz * t, C_WIDTH), _act_dtype(tq)),
        compiler_params=_cp("arbitrary", "arbitrary", "arbitrary"),
        name="fox_prompt",
    )(cq, ckb, cvb, frow, fcum)


def _page_suffix_kernel(lf_ref, rin_ref, tot_ref):
    pb = lf_ref.shape[0]
    x = lf_ref[...].reshape(pb * C_HEADS, PAGE)
    r_i = lax.broadcasted_iota(jnp.int32, (PAGE, PAGE), 0)
    c_i = lax.broadcasted_iota(jnp.int32, (PAGE, PAGE), 1)
    later = (r_i > c_i).astype(BF16)
    rin_ref[...] = _dot01_right(x, later).reshape(pb, C_HEADS, PAGE)
    tot = jnp.sum(x, axis=-1, keepdims=True)
    tot_ref[...] = jnp.broadcast_to(tot, (pb * C_HEADS, PAGE)).reshape(pb, C_HEADS, PAGE)


def _page_suffix(lft):
    n = lft.shape[0]
    pb = _tile(n, 64, 1)
    spec = pl.BlockSpec((pb, C_HEADS, PAGE), lambda i: (i, 0, 0))
    return pl.pallas_call(
        _page_suffix_kernel,
        grid=(n // pb,),
        in_specs=[spec],
        out_specs=[spec, spec],
        out_shape=[jax.ShapeDtypeStruct(lft.shape, F32)] * 2,
        compiler_params=_cp("arbitrary"),
        name="page_suffix",
    )(lft)


def _fox_sample_kernel(pt_ref, q_ref, gq_ref, gk_ref, kn_ref, vn_ref, *rest, pp, nsteps, tq):
    k_refs = rest[:pp]
    v_refs = rest[pp:2 * pp]
    rin_refs = rest[2 * pp:3 * pp]
    tot_refs = rest[3 * pp:4 * pp]
    o_ref = rest[4 * pp]
    m_scr, l_scr, acc_scr, carry_scr, madd_scr = rest[4 * pp + 1:]
    j = pl.program_id(1)
    rows = C_HEADS * tq
    cols = PAGE * C_HEADS

    @pl.when(j == 0)
    def _():
        m_scr[...] = jnp.full_like(m_scr, NEG_BIG)
        l_scr[...] = jnp.zeros_like(l_scr)
        acc_scr[...] = jnp.zeros_like(acc_scr)
        carry_scr[...] = jnp.zeros_like(carry_scr)
        r_i = lax.broadcasted_iota(jnp.int32, (rows, cols), 0)
        c_i = lax.broadcasted_iota(jnp.int32, (rows, cols), 1)
        madd_scr[...] = jnp.where(c_i % C_HEADS == r_i // tq, 0.0, NEG_BIG) + gq_ref[...]

    q = q_ref[...]

    def scores(k_blk, row_bias):
        k2 = k_blk.reshape(cols, C_DH).astype(BF16)
        return _dot_nt(q, k2) * C_SCALE + madd_scr[...] + row_bias

    def update(s_list, v_list):
        m_old = m_scr[...]
        m_new = m_old
        for s in s_list:
            m_new = jnp.maximum(m_new, jnp.max(s, axis=-1, keepdims=True))
        alpha = jnp.exp(m_old - m_new)
        l_new = alpha * l_scr[...]
        pv = None
        for s, v in zip(s_list, v_list):
            p = jnp.exp(s - m_new)
            l_new = l_new + jnp.sum(p, axis=-1, keepdims=True)
            d = _dot(p.astype(BF16), v.reshape(cols, C_DH).astype(BF16))
            pv = d if pv is None else pv + d
        m_scr[...] = m_new
        l_scr[...] = l_new
        acc_scr[...] = alpha * acc_scr[...] + pv

    carry = carry_scr[...]
    s_list = []
    for u in range(pp):
        s_list.append(scores(k_refs[u][...], rin_refs[u][...] + carry))
        carry = carry + tot_refs[u][...]
    carry_scr[...] = carry
    update(s_list, [v_refs[u][...] for u in range(pp)])

    @pl.when(j == nsteps - 1)
    def _():
        r_i = lax.broadcasted_iota(jnp.int32, (rows, cols), 0)
        c_i = lax.broadcasted_iota(jnp.int32, (rows, cols), 1)
        causal = jnp.where(c_i // C_HEADS <= r_i % tq, 0.0, NEG_BIG)
        update([scores(kn_ref[...], -gk_ref[...]) + causal], [vn_ref[...]])
        o_ref[...] = acc_scr[...] / l_scr[...]


def _fox_sample(cq, ck, cv, fcum, cache_k, cache_v, rin, tot, page_table, l, bsz, t):
    n_pages = page_table.shape[1]
    pp = 4 if n_pages % 4 == 0 else 1
    nsteps = n_pages // pp
    rows = C_HEADS * t
    cols = PAGE * C_HEADS
    q2 = cq.reshape(bsz, t, C_HEADS, C_DH).transpose(0, 2, 1, 3).reshape(bsz, rows, C_DH)
    g = fcum[:, :C_HEADS].reshape(bsz, t, C_HEADS)
    gq = g.transpose(0, 2, 1).reshape(bsz, rows, 1)
    gk = jnp.pad(g.reshape(bsz, 1, t * C_HEADS), ((0, 0), (0, 0), (0, cols - t * C_HEADS)))
    tok_pad = ((0, 0), (0, PAGE - t), (0, 0), (0, 0))
    kn = jnp.pad(ck.reshape(bsz, t, C_HEADS, C_DH), tok_pad)
    vn = jnp.pad(cv.reshape(bsz, t, C_HEADS, C_DH), tok_pad)

    def page_idx(b, j, pt, u):
        return pt[b * n_pages + n_pages - 1 - (j * pp + u)]

    def kv_spec(u):
        return pl.BlockSpec((None, None, PAGE, C_HEADS, C_DH),
                            lambda b, j, pt: (l, page_idx(b, j, pt, u), 0, 0, 0))

    def vec_spec(u):
        return pl.BlockSpec((None, None, 1, cols), lambda b, j, pt: (l, page_idx(b, j, pt, u), 0, 0))

    batch3 = lambda b, j, pt: (b, 0, 0)
    batch4 = lambda b, j, pt: (b, 0, 0, 0)
    in_specs = [pl.BlockSpec((None, rows, C_DH), batch3),
                pl.BlockSpec((None, rows, 1), batch3),
                pl.BlockSpec((None, 1, cols), batch3),
                pl.BlockSpec((None, PAGE, C_HEADS, C_DH), batch4),
                pl.BlockSpec((None, PAGE, C_HEADS, C_DH), batch4)]
    in_specs += [kv_spec(u) for u in range(pp)]
    in_specs += [kv_spec(u) for u in range(pp)]
    in_specs += [vec_spec(u) for u in range(pp)]
    in_specs += [vec_spec(u) for u in range(pp)]
    grid_spec = pltpu.PrefetchScalarGridSpec(
        num_scalar_prefetch=1,
        grid=(bsz, nsteps),
        in_specs=in_specs,
        out_specs=pl.BlockSpec((None, rows, C_DH), batch3),
        scratch_shapes=[pltpu.VMEM((rows, 1), F32), pltpu.VMEM((rows, 1), F32),
                        pltpu.VMEM((rows, C_DH), F32), pltpu.VMEM((1, cols), F32),
                        pltpu.VMEM((rows, cols), F32)],
    )
    out = pl.pallas_call(
        functools.partial(_fox_sample_kernel, pp=pp, nsteps=nsteps, tq=t),
        grid_spec=grid_spec,
        out_shape=jax.ShapeDtypeStruct((bsz, rows, C_DH), F32),
        compiler_params=_cp("arbitrary", "arbitrary"),
        name="fox_sample",
    )(page_table.reshape(-1), q2, gq, gk, kn, vn,
      *([cache_k] * pp), *([cache_v] * pp), *([rin] * pp), *([tot] * pp))
    return out.reshape(bsz, C_HEADS, t, C_DH).transpose(0, 2, 1, 3).reshape(bsz * t, C_WIDTH)


def _merge_kernel(h_ref, oa_ref, ob_ref, oc_ref, ga_ref, gb_ref, gc_ref, wa_ref, wb_ref, wc_ref, o_ref,
                  wab, wbb, wcb):
    @pl.when(pl.program_id(1) == 0)
    def _():
        wab[...] = wa_ref[...].astype(BF16)
        wbb[...] = wb_ref[...].astype(BF16)
        wcb[...] = wc_ref[...].astype(BF16)

    h = h_ref[...]
    out = _sigmoid(_dot_nt(h, ga_ref[...])) * _dot(oa_ref[...].astype(BF16), wab[...])
    out = out + _sigmoid(_dot_nt(h, gb_ref[...])) * _dot(ob_ref[...].astype(BF16), wbb[...])
    out = out + _sigmoid(_dot_nt(h, gc_ref[...])) * _dot(oc_ref[...].astype(BF16), wcb[...])
    o_ref[...] = out.astype(o_ref.dtype)


def _merge(h, oa, ob, oc, wg_t, wpa, wpb, wpc, l):
    m, d = h.shape
    tm = _row_tile(m, ROW_TILE // 2)
    tn = _tile(d, 256, LANES)
    nb = d // tn
    act = lambda w: pl.BlockSpec((tm, w), lambda j, i: (i, 0))
    wgt = lambda w: pl.BlockSpec((None, w, tn), lambda j, i: (l, 0, j))
    gate = lambda g: pl.BlockSpec((None, tn, d), lambda j, i: (l, g * nb + j, 0))
    return pl.pallas_call(
        _merge_kernel,
        grid=(nb, m // tm),
        in_specs=[act(d), act(A_WIDTH), act(B_WIDTH), act(C_WIDTH), gate(0), gate(1), gate(2),
                  wgt(A_WIDTH), wgt(B_WIDTH), wgt(C_WIDTH)],
        out_specs=pl.BlockSpec((tm, tn), lambda j, i: (i, j)),
        out_shape=jax.ShapeDtypeStruct((m, d), BF16),
        scratch_shapes=[pltpu.VMEM((A_WIDTH, tn), BF16), pltpu.VMEM((B_WIDTH, tn), BF16),
                        pltpu.VMEM((C_WIDTH, tn), BF16)],
        compiler_params=_cp("arbitrary", "arbitrary"),
        name="gated_merge",
    )(h, oa, ob, oc, wg_t, wg_t, wg_t, wpa, wpb, wpc)


def _prep_w_in(w_in):
    depth, d, _ = w_in.shape
    w_t = jnp.swapaxes(w_in, 1, 2)
    w_c = w_t[:, _O_C:_O_CF].astype(BF16)
    w_g = w_t[:, _O_G:].astype(BF16)
    pad = jnp.zeros((depth, LANES - C_HEADS - 2 * B_HEADS, d), BF16)
    w_s = jnp.concatenate([w_t[:, _O_CF:_O_G].astype(BF16), w_t[:, _O_BI:_O_C].astype(BF16), pad], axis=1)
    return w_t, w_c, w_g, w_s


def _layer(x, l, w, groups, past):
    (ffn1_norm, ffn1_w1, ffn1_w3, ffn1_w2, mix_norm, w_t, w_c, w_g, w_s, lb_all, hgrn_norm, gate_bias,
     mlstm_norm, w_proj_a, w_proj_b, w_proj_c, w_out, ffn2_norm, ffn2_w1, ffn2_w3, ffn2_w2) = w

    h = _rmsnorm(x, ffn1_norm[l], BF16)
    x = _down(_ffn_up(h, ffn1_w1, ffn1_w3, l), ffn1_w2, x, l, 0.5)

    h = _rmsnorm(x, mix_norm[l], BF16)
    za = _proj(h, w_t, l, 0, 4 * A_WIDTH, F32)
    zb = _proj(h, w_t, l, _O_B, ZB_W, F32)
    zs = _proj(h, w_s, l, 0, LANES, F32)
    cq = _proj(h, w_c, l, 0, C_WIDTH, BF16)
    ck = _proj(h, w_c, l, C_WIDTH, C_WIDTH, F32)
    cv = _proj(h, w_c, l, 2 * C_WIDTH, C_WIDTH, F32)

    branch_out = []
    states_out = []
    for gi, (bsz, t, row0, (s_a0, c0, n0, m0)) in enumerate(groups):
        rows = slice(row0, row0 + bsz * t)
        oa, s_a = _hgrn(za, lb_all[l], hgrn_norm[l], s_a0, bsz, t, row0)
        if gi == 0:
            gact, fcum, frow = _gates(zs, gate_bias[l], bsz, t, row0, True)
        else:
            gact, fcum = _gates(zs, gate_bias[l], bsz, t, row0, False)
        ob, c_b, n_b, m_b = _mlstm(zb, gact, mlstm_norm[l], c0, n0, m0, bsz, t, row0)
        if gi == 0:
            assert row0 == 0
            oc = _fox_prompt(cq, ck.astype(BF16), cv.astype(BF16), frow, fcum, bsz, t)
        else:
            oc = _fox_sample(cq[rows], ck[rows], cv[rows], fcum, *past, l, bsz, t)
        branch_out.append((oa.astype(BF16), ob.astype(BF16), oc.astype(BF16)))
        states_out.append((ck[rows].reshape(bsz, t, C_HEADS, C_DH), cv[rows].reshape(bsz, t, C_HEADS, C_DH),
                           gact[:, :C_HEADS].reshape(bsz, t, C_HEADS), s_a, c_b,
                           n_b.reshape(bsz, B_HEADS, B_DK), m_b[:, :, 0, 0]))
    oa, ob, oc = (jnp.concatenate(parts, axis=0) for parts in zip(*branch_out))

    merged = _merge(h, oa, ob, oc, w_g, w_proj_a, w_proj_b, w_proj_c, l)
    x = _down(merged, w_out, x, l, 1.0)

    h = _rmsnorm(x, ffn2_norm[l], BF16)
    x = _down(_ffn_up(h, ffn2_w1, ffn2_w3, l), ffn2_w2, x, l, 0.5)
    return x, states_out


def kernel(x_prompt, x_sample, cache_k, cache_v, cache_logf, state_hgrn, state_mlstm_c, state_mlstm_n,
           state_mlstm_m, page_table, ffn1_norm, ffn1_w1, ffn1_w3, ffn1_w2, mix_norm, w_in, hgrn_lb,
           hgrn_norm, mlstm_b_i, mlstm_b_f, mlstm_norm, fox_b_f, w_proj_a, w_proj_b, w_proj_c, w_out,
           ffn2_norm, ffn2_w1, ffn2_w3, ffn2_w2, final_norm):
    depth = w_in.shape[0]
    bp, tp, d = x_prompt.shape
    bs, ts, _ = x_sample.shape
    n_pool = cache_k.shape[1]

    lb_all = _lower_bounds(hgrn_lb)
    w_t, w_c, w_g, w_s = _prep_w_in(w_in)
    gate_bias = jnp.concatenate(
        [fox_b_f, mlstm_b_i, mlstm_b_f, jnp.zeros((depth, LANES - C_HEADS - 2 * B_HEADS), F32)],
        axis=-1).reshape(depth, 1, LANES).astype(F32)
    w = (ffn1_norm, ffn1_w1, ffn1_w3, ffn1_w2.astype(BF16), mix_norm, w_t, w_c, w_g, w_s, lb_all, hgrn_norm,
         gate_bias, mlstm_norm, w_proj_a, w_proj_b, w_proj_c, w_out, ffn2_norm, ffn2_w1, ffn2_w3,
         ffn2_w2.astype(BF16))

    lft = jnp.swapaxes(cache_logf.astype(F32), -1, -2).reshape(depth * n_pool, C_HEADS, PAGE)
    flat = lambda a: jnp.swapaxes(a, -1, -2).reshape(depth, n_pool, 1, PAGE * C_HEADS)
    rin, tot = (flat(a) for a in _page_suffix(lft))

    zero_states = (jnp.zeros((bp, A_HEADS, A_DK, A_DV), F32), jnp.zeros((bp, B_HEADS, B_DK, B_DV), F32),
                   jnp.zeros((bp, B_HEADS, B_DK), F32), jnp.zeros((bp, B_HEADS), F32))

    mp = bp * tp
    x = jnp.concatenate([x_prompt.reshape(mp, d), x_sample.reshape(bs * ts, d)], axis=0)
    p_states = [[] for _ in range(7)]
    s_states = [[] for _ in range(7)]
    for l in range(depth):
        groups = ((bp, tp, 0, zero_states),
                  (bs, ts, mp, (state_hgrn[l], state_mlstm_c[l], state_mlstm_n[l], state_mlstm_m[l])))
        x, (st_p, st_s) = _layer(x, l, w, groups, (cache_k, cache_v, rin, tot, page_table))
        for i in range(7):
            p_states[i].append(st_p[i])
            s_states[i].append(st_s[i])

    y = _rmsnorm(x, final_norm, F32)
    outs_p = [jnp.stack(s, axis=0) for s in p_states]
    outs_s = [jnp.stack(s, axis=0) for s in s_states]
    return (y[:mp].reshape(bp, tp, d), y[mp:].reshape(bs, ts, d), *outs_p, *outs_s)
```

```python
import functools

import jax
import jax.numpy as jnp
from jax import lax
from jax.experimental import pallas as pl
from jax.experimental.pallas import tpu as pltpu

F32 = jnp.float32
BF16 = jnp.bfloat16

A_HEADS, A_DK, A_DV = 8, 128, 128
B_HEADS, B_DK, B_DV = 4, 128, 256
C_HEADS, C_DH = 16, 128
A_WIDTH = A_HEADS * A_DV
B_WIDTH = B_HEADS * B_DV
C_WIDTH = C_HEADS * C_DH
GATE_SOFTCAP = 15.0
C_SCALE = C_DH ** -0.5
EPS = 1e-6
NEG_BIG = -1e30
TINY = 1e-30
PAGE = 128

LANES = 128
VMEM_LIMIT = 56 * 1024 * 1024

_O_B = 4 * A_WIDTH
_O_BI = _O_B + 2 * B_HEADS * B_DK + 2 * B_WIDTH
_O_BF = _O_BI + B_HEADS
_O_C = _O_BF + B_HEADS
_O_CF = _O_C + 3 * C_WIDTH
_O_G = _O_CF + C_HEADS

ZB_W = 2 * B_HEADS * B_DK + 2 * B_WIDTH
LANE_I = C_HEADS
LANE_F = C_HEADS + B_HEADS
SCAN_HEADS_PER_STEP = 4


def _cp(*sem):
    return pltpu.CompilerParams(dimension_semantics=sem, vmem_limit_bytes=VMEM_LIMIT)


def _tile(n, pref, mult=8):
    if n <= pref:
        return n
    t = (pref // mult) * mult
    while t >= mult:
        if n % t == 0:
            return t
        t -= mult
    return n


ROW_TILE = 1376
ACC_BYTES = 16 * 1024 * 1024


def _row_tile(m, cap=ROW_TILE):
    return _tile(m, cap, 16)


def _act_dtype(rows):
    return BF16 if rows % 16 == 0 else F32


def _sigmoid(x):
    return jax.nn.sigmoid(x)


def _log_sigmoid(x):
    return jnp.minimum(x, 0.0) - jnp.log(1.0 + jnp.exp(-jnp.abs(x)))


def _dot(a, b):
    return jnp.dot(a, b, preferred_element_type=F32)


def _dot_nt(a, b):
    return lax.dot_general(a, b, (((1,), (1,)), ((), ())), preferred_element_type=F32)


def _dot_tn(a, b):
    return lax.dot_general(a, b, (((0,), (0,)), ((), ())), preferred_element_type=F32)


def _split3(x):
    hi = x.astype(BF16)
    r1 = x - hi.astype(F32)
    mid = r1.astype(BF16)
    lo = (r1 - mid.astype(F32)).astype(BF16)
    return hi, mid, lo


def _dot01_left(m01, x):
    hi, mid, lo = _split3(x)
    return (_dot(m01, hi) + _dot(m01, mid)) + _dot(m01, lo)


def _dot01_right(x, m01):
    hi, mid, lo = _split3(x)
    return (_dot(hi, m01) + _dot(mid, m01)) + _dot(lo, m01)


def _tri_incl(n):
    r = lax.broadcasted_iota(jnp.int32, (n, n), 0)
    c = lax.broadcasted_iota(jnp.int32, (n, n), 1)
    return r >= c


def _cumsum_rows(x):
    n = x.shape[0]
    return _dot01_left(_tri_incl(n).astype(BF16), x)


def _lane_col(x, lane):
    idx = lax.broadcasted_iota(jnp.int32, x.shape, 1)
    return jnp.sum(jnp.where(idx == lane, x, 0.0), axis=-1, keepdims=True)


def _col_to_row(col):
    n = col.shape[0]
    r = lax.broadcasted_iota(jnp.int32, (n, n), 0)
    c = lax.broadcasted_iota(jnp.int32, (n, n), 1)
    return jnp.sum(jnp.where(r == c, jnp.broadcast_to(col, (n, n)), 0.0), axis=0, keepdims=True)


def _lb_kernel(x_ref, o_ref):
    x = x_ref[...]
    e = jnp.exp(x - jnp.max(x, axis=0, keepdims=True))
    w = e / jnp.sum(e, axis=0, keepdims=True)
    run = jnp.zeros_like(w[0:1])
    for l in range(x.shape[0]):
        o_ref[l:l + 1, :] = jnp.clip(run, 0.0, 1.0)
        run = run + w[l:l + 1]


def _lower_bounds(hgrn_lb):
    return pl.pallas_call(
        _lb_kernel, out_shape=jax.ShapeDtypeStruct(hgrn_lb.shape, F32), name="hgrn_lower_bounds",
    )(hgrn_lb.astype(F32))


def _rmsnorm_kernel(x_ref, g_ref, o_ref):
    x = x_ref[...]
    y = x * lax.rsqrt(jnp.mean(x * x, axis=-1, keepdims=True) + EPS)
    o_ref[...] = (y * g_ref[...]).astype(o_ref.dtype)


def _rmsnorm(x, g, out_dtype):
    m, d = x.shape
    tm = _tile(m, 256)
    return pl.pallas_call(
        _rmsnorm_kernel,
        grid=(m // tm,),
        in_specs=[pl.BlockSpec((tm, d), lambda i: (i, 0)), pl.BlockSpec((1, d), lambda i: (0, 0))],
        out_specs=pl.BlockSpec((tm, d), lambda i: (i, 0)),
        out_shape=jax.ShapeDtypeStruct((m, d), out_dtype),
        compiler_params=_cp("arbitrary"),
        name="rmsnorm",
    )(x, g.reshape(1, d))


def _ffn_up_kernel(h_ref, w1_ref, w3_ref, o_ref, w1b, w3b):
    @pl.when(pl.program_id(1) == 0)
    def _():
        w1b[...] = w1_ref[...].astype(BF16)
        w3b[...] = w3_ref[...].astype(BF16)

    h = h_ref[...]
    a = _dot(h, w1b[...])
    b = _dot(h, w3b[...])
    o_ref[...] = (a * _sigmoid(a) * b).astype(o_ref.dtype)


def _ffn_up(h, w1, w3, l):
    m, d = h.shape
    f = w1.shape[-1]
    tm = _row_tile(m)
    tn = _tile(f, 256, LANES)
    return pl.pallas_call(
        _ffn_up_kernel,
        grid=(f // tn, m // tm),
        in_specs=[pl.BlockSpec((tm, d), lambda j, i: (i, 0)),
                  pl.BlockSpec((None, d, tn), lambda j, i: (l, 0, j)),
                  pl.BlockSpec((None, d, tn), lambda j, i: (l, 0, j))],
        out_specs=pl.BlockSpec((tm, tn), lambda j, i: (i, j)),
        out_shape=jax.ShapeDtypeStruct((m, f), BF16),
        scratch_shapes=[pltpu.VMEM((d, tn), BF16), pltpu.VMEM((d, tn), BF16)],
        compiler_params=_cp("arbitrary", "arbitrary"),
        name="ffn_up",
    )(h, w1, w3)


def _down_kernel(a_ref, w_ref, r_ref, o_ref, *acc, scale, nk):
    k = pl.program_id(1)
    j = pl.program_id(2)
    p = _dot(a_ref[...], w_ref[...].astype(BF16))
    if nk == 1:
        o_ref[...] = r_ref[...] + scale * p
        return
    acc_ref = acc[0]

    @pl.when(k == 0)
    def _():
        acc_ref[j] = p

    @pl.when((k > 0) & (k < nk - 1))
    def _():
        acc_ref[j] += p

    @pl.when(k == nk - 1)
    def _():
        o_ref[...] = r_ref[...] + scale * (acc_ref[j] + p)


def _down(a, w, res, l, scale):
    m, kdim = a.shape
    n = w.shape[-1]
    tn = _tile(n, 256, LANES)
    tk = kdim if kdim <= 4096 else kdim // 2
    assert kdim % tk == 0 and tk % LANES == 0
    nk = kdim // tk
    tm = _row_tile(m) if nk == 1 else _row_tile(m, min(ROW_TILE, ACC_BYTES // (4 * n)))
    nj = n // tn
    out_idx = lambda i, k, j: (i, jnp.where(k == nk - 1, j, 0))
    return pl.pallas_call(
        functools.partial(_down_kernel, scale=scale, nk=nk),
        grid=(m // tm, nk, nj),
        in_specs=[pl.BlockSpec((tm, tk), lambda i, k, j: (i, k)),
                  pl.BlockSpec((None, tk, tn), lambda i, k, j: (l, k, j)),
                  pl.BlockSpec((tm, tn), out_idx)],
        out_specs=pl.BlockSpec((tm, tn), out_idx),
        out_shape=jax.ShapeDtypeStruct((m, n), F32),
        scratch_shapes=[pltpu.VMEM((nj, tm, tn), F32)] if nk > 1 else [],
        compiler_params=_cp("arbitrary", "arbitrary", "arbitrary"),
        name="down_residual",
    )(a, w, res)


def _proj_kernel(h_ref, w_ref, o_ref, *wb):
    if wb:
        @pl.when(pl.program_id(1) == 0)
        def _():
            wb[0][...] = w_ref[...].astype(BF16)
        w = wb[0][...]
    else:
        w = w_ref[...]
    o_ref[...] = _dot_nt(h_ref[...], w).astype(o_ref.dtype)


def _proj(h, w_t, l, row_off, nrows, out_dtype):
    m, d = h.shape
    tm = _row_tile(m)
    tn = _tile(nrows, 512, LANES)
    assert row_off % tn == 0 and nrows % tn == 0
    rb = row_off // tn
    scratch = [pltpu.VMEM((tn, d), BF16)] if w_t.dtype != BF16 else []
    return pl.pallas_call(
        _proj_kernel,
        grid=(nrows // tn, m // tm),
        in_specs=[pl.BlockSpec((tm, d), lambda j, i: (i, 0)),
                  pl.BlockSpec((None, tn, d), lambda j, i: (l, rb + j, 0))],
        out_specs=pl.BlockSpec((tm, tn), lambda j, i: (i, j)),
        out_shape=jax.ShapeDtypeStruct((m, nrows), out_dtype),
        scratch_shapes=scratch,
        compiler_params=_cp("arbitrary", "arbitrary"),
        name="in_proj",
    )(h, w_t)


def _proj_kv_kernel(h_ref, w_ref, *rest):
    o_ref, ob_ref = rest[-2:]
    y = _dot_nt(h_ref[...], w_ref[...])
    o_ref[...] = y
    ob_ref[...] = y.astype(ob_ref.dtype)


def _proj_kv(h, w_t, l, row_off, nrows, stack, depth):
    m, d = h.shape
    tm = _row_tile(m)
    tn = _tile(nrows, 512, LANES)
    assert row_off % tn == 0 and nrows % tn == 0 and w_t.dtype == BF16
    rb = row_off // tn
    in_specs = [pl.BlockSpec((tm, d), lambda j, i: (i, 0)),
                pl.BlockSpec((None, tn, d), lambda j, i: (l, rb + j, 0))]
    args = [h, w_t]
    aliases = {}
    if stack is not None:
        in_specs.append(pl.BlockSpec(memory_space=pl.ANY))
        args.append(stack)
        aliases = {2: 0}
    return pl.pallas_call(
        _proj_kv_kernel,
        grid=(nrows // tn, m // tm),
        in_specs=in_specs,
        out_specs=[pl.BlockSpec((None, tm, tn), lambda j, i: (l, i, j)),
                   pl.BlockSpec((tm, tn), lambda j, i: (i, j))],
        out_shape=[jax.ShapeDtypeStruct((depth, m, nrows), F32), jax.ShapeDtypeStruct((m, nrows), BF16)],
        input_output_aliases=aliases,
        compiler_params=_cp("arbitrary", "arbitrary"),
        name="in_proj_kv",
    )(*args)


def _gates_kernel(s_ref, b_ref, act_ref, cum_ref, *rest, emit_rows):
    if emit_rows:
        row_ref, carry = rest
    else:
        (carry,) = rest

    @pl.when(pl.program_id(1) == 0)
    def _():
        carry[...] = jnp.zeros_like(carry)

    x = s_ref[...] + b_ref[...]
    lane = lax.broadcasted_iota(jnp.int32, x.shape, 1)
    capped = GATE_SOFTCAP * jnp.tanh(x / GATE_SOFTCAP)
    ls = _log_sigmoid(jnp.where(lane < LANE_I, x, capped))
    act = jnp.where((lane >= LANE_I) & (lane < LANE_F), capped, ls)
    act = jnp.where(lane < LANE_F + B_HEADS, act, 0.0)
    act_ref[...] = act
    c = _cumsum_rows(act) + carry[...]
    cum_ref[...] = c
    carry[...] = c[c.shape[0] - 1:, :]
    if emit_rows:
        row_ref[...] = c.T


def _gates(zs, bias, bsz, t, row0, emit_rows):
    m = bsz * t
    tb = _tile(t, LANES)
    nb = t // tb
    out_shape = [jax.ShapeDtypeStruct((m, LANES), F32), jax.ShapeDtypeStruct((m, LANES), F32)]
    out_specs = [pl.BlockSpec((tb, LANES), lambda b, j: (b * nb + j, 0)),
                 pl.BlockSpec((tb, LANES), lambda b, j: (b * nb + j, 0))]
    if emit_rows:
        out_shape.append(jax.ShapeDtypeStruct((bsz, LANES, t), F32))
        out_specs.append(pl.BlockSpec((None, LANES, tb), lambda b, j: (b, 0, j)))
    return pl.pallas_call(
        functools.partial(_gates_kernel, emit_rows=emit_rows),
        grid=(bsz, nb),
        in_specs=[pl.BlockSpec((tb, LANES), lambda b, j: (row0 // tb + b * nb + j, 0)),
                  pl.BlockSpec((1, LANES), lambda b, j: (0, 0))],
        out_specs=out_specs,
        out_shape=out_shape,
        scratch_shapes=[pltpu.VMEM((1, LANES), F32)],
        compiler_params=_cp("arbitrary", "arbitrary"),
        name="gate_activations",
    )(zs, bias)


def _hgrn_chunk(aq, af, ai, ag, lb, hn, s_prev, sub):
    c_len = aq.shape[0]
    q = aq * _sigmoid(aq)
    f = lb + (1.0 - lb) * _sigmoid(af)
    lf = jnp.log(jnp.maximum(f, TINY))
    k = 1.0 - f
    v = ai
    g = _cumsum_rows(lf)
    o = _dot((q * jnp.exp(g)).astype(BF16), s_prev.astype(BF16))
    rows = lax.broadcasted_iota(jnp.int32, (sub, A_DK), 0)
    parts = []
    for i0 in range(0, c_len, sub):
        g_i = g[i0:i0 + sub]
        q_i = q[i0:i0 + sub]
        o_i = o[i0:i0 + sub]
        if i0 > 0:
            r = g[i0 - 1:i0]
            q_t = (q_i * jnp.exp(g_i - r)).astype(BF16)
            k_t = (k[:i0] * jnp.exp(r - g[:i0])).astype(BF16)
            a = _dot_nt(q_t, k_t)
            o_i = o_i + _dot(a.astype(BF16), v[:i0].astype(BF16))
        for s in range(sub):
            e = jnp.exp(jnp.minimum(g_i - g_i[s:s + 1], 0.0))
            w = jnp.where(rows >= s, q_i * (k[i0 + s:i0 + s + 1] * e), 0.0)
            o_i = o_i + jnp.sum(w, axis=-1, keepdims=True) * v[i0 + s:i0 + s + 1]
        parts.append(o_i)
    o = parts[0] if len(parts) == 1 else jnp.concatenate(parts, axis=0)
    g_last = g[c_len - 1:c_len]
    k_dec = (k * jnp.exp(g_last - g)).astype(BF16)
    dec_col = _row_to_col(jnp.exp(g_last))
    s_new = dec_col * s_prev + _dot_tn(k_dec, v.astype(BF16))
    y = o * lax.rsqrt(jnp.mean(o * o, axis=-1, keepdims=True) + EPS) * hn
    return y * (ag * _sigmoid(ag)), s_new


def _row_to_col(row):
    n = row.shape[1]
    r = lax.broadcasted_iota(jnp.int32, (n, n), 0)
    c = lax.broadcasted_iota(jnp.int32, (n, n), 1)
    return jnp.sum(jnp.where(r == c, jnp.broadcast_to(row, (n, n)), 0.0), axis=1, keepdims=True)


def _hgrn_kernel(aq_ref, af_ref, ai_ref, ag_ref, lb_ref, hn_ref, s0_ref, o_ref, sfin_ref, s_scr,
                 *, chunk, sub, nchunks, nblocks):
    tb = pl.program_id(2)

    @pl.when(tb == 0)
    def _():
        s_scr[...] = s0_ref[...]

    hn = hn_ref[...]

    def body(ci, carry):
        r0 = pl.multiple_of(ci * chunk, chunk)
        sl = pl.ds(r0, chunk)
        for hh in range(SCAN_HEADS_PER_STEP):
            cs = slice(hh * A_DK, (hh + 1) * A_DK)
            out, s_new = _hgrn_chunk(aq_ref[sl, cs], af_ref[sl, cs], ai_ref[sl, cs], ag_ref[sl, cs],
                                     lb_ref[:, cs], hn, s_scr[hh], sub)
            s_scr[hh] = s_new
            o_ref[sl, cs] = out.astype(o_ref.dtype)
        return carry

    lax.fori_loop(0, nchunks, body, 0)

    @pl.when(tb == nblocks - 1)
    def _():
        sfin_ref[...] = s_scr[...]


def _hgrn(za, lb, hnorm, s0, bsz, t, row0):
    chunk = min(t, 64)
    sub = min(chunk, 16)
    tg = _tile(t, 256, chunk)
    nblocks = t // tg
    hp = SCAN_HEADS_PER_STEP
    ng = A_HEADS // hp

    def seg(s):
        return pl.BlockSpec((tg, hp * A_DK), lambda b, h, j: (row0 // tg + b * nblocks + j, s * ng + h))

    state = pl.BlockSpec((None, hp, A_DK, A_DV), lambda b, h, j: (b, h, 0, 0))
    return pl.pallas_call(
        functools.partial(_hgrn_kernel, chunk=chunk, sub=sub, nchunks=tg // chunk, nblocks=nblocks),
        grid=(bsz, ng, nblocks),
        in_specs=[seg(0), seg(1), seg(2), seg(3),
                  pl.BlockSpec((1, hp * A_DK), lambda b, h, j: (0, h)),
                  pl.BlockSpec((1, A_DV), lambda b, h, j: (0, 0)),
                  state],
        out_specs=[pl.BlockSpec((tg, hp * A_DV), lambda b, h, j: (b * nblocks + j, h)), state],
        out_shape=[jax.ShapeDtypeStruct((bsz * t, A_WIDTH), _act_dtype(tg)),
                   jax.ShapeDtypeStruct((bsz, A_HEADS, A_DK, A_DV), F32)],
        scratch_shapes=[pltpu.VMEM((hp, A_DK, A_DV), F32)],
        compiler_params=_cp("arbitrary", "arbitrary", "arbitrary"),
        name="hgrn2",
    )(za, za, za, za, lb.reshape(1, A_WIDTH), hnorm.reshape(1, A_DV), s0)


def _mlstm_chunk(q, k, v, og, gact, head, mn, c_prev, n_prev, m_prev):
    c_len = q.shape[0]
    logi = _lane_col(gact, LANE_I + head)
    fcum = _lane_col(_cumsum_rows(gact), LANE_F + head)
    m0 = m_prev[:, 0:1]
    causal = _tri_incl(c_len)
    dmat = jnp.where(causal, fcum - _col_to_row(fcum) + _col_to_row(logi), NEG_BIG)
    b_inter = fcum + m0
    m_t = jnp.maximum(b_inter, jnp.max(dmat, axis=1, keepdims=True))
    w_inter = jnp.exp(b_inter - m_t)
    qb = q.astype(BF16)
    ks = k * (B_DK ** -0.5)
    pw = _dot_nt(qb, ks.astype(BF16)) * jnp.exp(dmat - m_t)
    num = w_inter * _dot(qb, c_prev.astype(BF16)) + _dot(pw.astype(BF16), v.astype(BF16))
    den = w_inter * jnp.sum(q * n_prev, axis=-1, keepdims=True) + jnp.sum(pw, axis=-1, keepdims=True)
    h_out = num / jnp.maximum(jnp.abs(den), jnp.exp(-m_t))
    f_last = fcum[c_len - 1:c_len]
    m_new = m_t[c_len - 1:c_len]
    decay = jnp.exp(f_last + m0 - m_new)
    w_st = jnp.exp(f_last - fcum + logi - m_new)
    kw = ks * w_st
    c_new = decay * c_prev + _dot_tn(kw.astype(BF16), v.astype(BF16))
    n_new = decay * n_prev + jnp.sum(kw, axis=0, keepdims=True)
    y = h_out * lax.rsqrt(jnp.mean(h_out * h_out, axis=-1, keepdims=True) + EPS) * mn
    return y * _sigmoid(og), c_new, n_new, jnp.broadcast_to(m_new, m_prev.shape)


def _mlstm_kernel(q_ref, k_ref, v_ref, og_ref, g_ref, mn_ref, c0_ref, n0_ref, m0_ref,
                  o_ref, cf_ref, nf_ref, mf_ref, c_scr, n_scr, m_scr, *, chunk, nchunks, nblocks):
    hp = SCAN_HEADS_PER_STEP
    hg = pl.program_id(1)
    tb = pl.program_id(2)

    @pl.when(tb == 0)
    def _():
        c_scr[...] = c0_ref[...]
        n_scr[...] = n0_ref[...]
        m_scr[...] = m0_ref[...]

    mn = mn_ref[...]

    def body(ci, carry):
        r0 = pl.multiple_of(ci * chunk, chunk)
        sl = pl.ds(r0, chunk)
        gact = g_ref[sl, :]
        for hh in range(hp):
            ks = slice(hh * B_DK, (hh + 1) * B_DK)
            vs = slice(hh * B_DV, (hh + 1) * B_DV)
            out, c_new, n_new, m_new = _mlstm_chunk(q_ref[sl, ks], k_ref[sl, ks], v_ref[sl, vs], og_ref[sl, vs],
                                                    gact, hg * hp + hh, mn, c_scr[hh], n_scr[hh], m_scr[hh])
            c_scr[hh] = c_new
            n_scr[hh] = n_new
            m_scr[hh] = m_new
            o_ref[sl, vs] = out.astype(o_ref.dtype)
        return carry

    lax.fori_loop(0, nchunks, body, 0)

    @pl.when(tb == nblocks - 1)
    def _():
        cf_ref[...] = c_scr[...]
        nf_ref[...] = n_scr[...]
        mf_ref[...] = m_scr[...]


def _mlstm(zb, gact, mnorm, c0, n0, m0, bsz, t, row0):
    chunk = min(t, 64)
    tg = _tile(t, 256, chunk)
    nblocks = t // tg
    hp = SCAN_HEADS_PER_STEP
    ng = B_HEADS // hp
    row = lambda b, h, j: b * nblocks + j
    zrow = lambda b, h, j: row0 // tg + row(b, h, j)
    st4 = lambda b, h, j: (b, h, 0, 0)
    return pl.pallas_call(
        functools.partial(_mlstm_kernel, chunk=chunk, nchunks=tg // chunk, nblocks=nblocks),
        grid=(bsz, ng, nblocks),
        in_specs=[pl.BlockSpec((tg, hp * B_DK), lambda b, h, j: (zrow(b, h, j), h)),
                  pl.BlockSpec((tg, hp * B_DK), lambda b, h, j: (zrow(b, h, j), ng + h)),
                  pl.BlockSpec((tg, hp * B_DV), lambda b, h, j: (zrow(b, h, j), ng + h)),
                  pl.BlockSpec((tg, hp * B_DV), lambda b, h, j: (zrow(b, h, j), 2 * ng + h)),
                  pl.BlockSpec((tg, LANES), lambda b, h, j: (row(b, h, j), 0)),
                  pl.BlockSpec((1, B_DV), lambda b, h, j: (0, 0)),
                  pl.BlockSpec((None, hp, B_DK, B_DV), st4),
                  pl.BlockSpec((None, hp, 1, B_DK), st4),
                  pl.BlockSpec((None, hp, 1, LANES), st4)],
        out_specs=[pl.BlockSpec((tg, hp * B_DV), lambda b, h, j: (row(b, h, j), h)),
                   pl.BlockSpec((None, hp, B_DK, B_DV), st4),
                   pl.BlockSpec((None, hp, 1, B_DK), st4),
                   pl.BlockSpec((None, hp, 1, LANES), st4)],
        out_shape=[jax.ShapeDtypeStruct((bsz * t, B_WIDTH), _act_dtype(tg)),
                   jax.ShapeDtypeStruct((bsz, B_HEADS, B_DK, B_DV), F32),
                   jax.ShapeDtypeStruct((bsz, B_HEADS, 1, B_DK), F32),
                   jax.ShapeDtypeStruct((bsz, B_HEADS, 1, LANES), F32)],
        scratch_shapes=[pltpu.VMEM((hp, B_DK, B_DV), F32), pltpu.VMEM((hp, 1, B_DK), F32),
                        pltpu.VMEM((hp, 1, LANES), F32)],
        compiler_params=_cp("arbitrary", "arbitrary", "arbitrary"),
        name="mlstm",
    )(zb, zb, zb, zb, gact, mnorm.reshape(1, B_DV), c0,
      n0.reshape(bsz, B_HEADS, 1, B_DK),
      jnp.broadcast_to(m0[:, :, None, None], (bsz, B_HEADS, 1, LANES)))


FOX_HEADS_PER_STEP = 2


def _fox_prompt_kernel(q_ref, k_ref, v_ref, frow_ref, fcol_ref, o_ref, *, tq, tk):
    hp = FOX_HEADS_PER_STEP
    hg = pl.program_id(1)
    qi = pl.program_id(2)
    fc = fcol_ref[...]
    heads = []
    for hh in range(hp):
        head = hg * hp + hh
        heads.append((head, q_ref[:, hh * C_DH:(hh + 1) * C_DH], _lane_col(fc, head)))
    qpos = qi * tq + lax.broadcasted_iota(jnp.int32, (tq, tk), 0)
    kiota = lax.broadcasted_iota(jnp.int32, (tq, tk), 1)

    def step(j, carry, masked):
        k0 = pl.multiple_of(j * tk, tk)
        new = []
        for hh, (head, q, fcol) in enumerate(heads):
            m, l, acc = carry[hh]
            kj = k_ref[pl.ds(k0, tk), hh * C_DH:(hh + 1) * C_DH]
            vj = v_ref[pl.ds(k0, tk), hh * C_DH:(hh + 1) * C_DH]
            fr = frow_ref[pl.ds(head, 1), pl.ds(k0, tk)]
            s = _dot_nt(q, kj) * C_SCALE + (fcol - fr)
            if masked:
                s = jnp.where(k0 + kiota <= qpos, s, NEG_BIG)
            m_new = jnp.maximum(m, jnp.max(s, axis=-1, keepdims=True))
            alpha = jnp.exp(m - m_new)
            p = jnp.exp(s - m_new)
            l = alpha * l + jnp.sum(p, axis=-1, keepdims=True)
            acc = alpha * acc + _dot(p.astype(BF16), vj)
            new.append((m_new, l, acc))
        return tuple(new)

    one = (jnp.full((tq, 1), NEG_BIG, F32), jnp.zeros((tq, 1), F32), jnp.zeros((tq, C_DH), F32))
    ndiag = tq // tk
    nfull = qi * ndiag
    carry = lax.fori_loop(0, nfull, lambda j, c: step(j, c, False), (one,) * hp)
    for dblk in range(ndiag):
        carry = step(nfull + dblk, carry, True)
    for hh in range(hp):
        m, l, acc = carry[hh]
        o_ref[:, hh * C_DH:(hh + 1) * C_DH] = (acc / l).astype(o_ref.dtype)


def _fox_prompt(cq, ckb, cvb, frow, fcum, bsz, t):
    tq = _tile(t, 512, LANES)
    tk = _tile(tq, 256, LANES)
    nq = t // tq
    hp = FOX_HEADS_PER_STEP
    w = hp * C_DH
    return pl.pallas_call(
        functools.partial(_fox_prompt_kernel, tq=tq, tk=tk),
        grid=(bsz, C_HEADS // hp, nq),
        in_specs=[pl.BlockSpec((tq, w), lambda b, h, i: (b * nq + i, h)),
                  pl.BlockSpec((t, w), lambda b, h, i: (b, h)),
                  pl.BlockSpec((t, w), lambda b, h, i: (b, h)),
                  pl.BlockSpec((None, C_HEADS, t), lambda b, h, i: (b, 0, 0)),
                  pl.BlockSpec((tq, LANES), lambda b, h, i: (b * nq + i, 0))],
        out_specs=pl.BlockSpec((tq, w), lambda b, h, i: (b * nq + i, h)),
        out_shape=jax.ShapeDtypeStruct((bsz * t, C_WIDTH), _act_dtype(tq)),
        compiler_params=_cp("arbitrary", "arbitrary", "arbitrary"),
        name="fox_prompt",
    )(cq, ckb, cvb, frow, fcum)


def _page_suffix_kernel(lf_ref, rin_ref, tot_ref):
    pb = lf_ref.shape[0]
    x = lf_ref[...].reshape(pb * C_HEADS, PAGE)
    r_i = lax.broadcasted_iota(jnp.int32, (PAGE, PAGE), 0)
    c_i = lax.broadcasted_iota(jnp.int32, (PAGE, PAGE), 1)
    later = (r_i > c_i).astype(BF16)
    rin_ref[...] = _dot01_right(x, later).reshape(pb, C_HEADS, PAGE)
    tot = jnp.sum(x, axis=-1, keepdims=True)
    tot_ref[...] = jnp.broadcast_to(tot, (pb * C_HEADS, PAGE)).reshape(pb, C_HEADS, PAGE)


def _page_suffix(lft):
    n = lft.shape[0]
    pb = _tile(n, 64, 1)
    spec = pl.BlockSpec((pb, C_HEADS, PAGE), lambda i: (i, 0, 0))
    return pl.pallas_call(
        _page_suffix_kernel,
        grid=(n // pb,),
        in_specs=[spec],
        out_specs=[spec, spec],
        out_shape=[jax.ShapeDtypeStruct(lft.shape, F32)] * 2,
        compiler_params=_cp("arbitrary"),
        name="page_suffix",
    )(lft)


def _fox_sample_kernel(pt_ref, q_ref, gq_ref, gk_ref, kn_ref, vn_ref, *rest, pp, nsteps, tq):
    k_refs = rest[:pp]
    v_refs = rest[pp:2 * pp]
    rin_refs = rest[2 * pp:3 * pp]
    tot_refs = rest[3 * pp:4 * pp]
    o_ref = rest[4 * pp]
    m_scr, l_scr, acc_scr, carry_scr, madd_scr = rest[4 * pp + 1:]
    j = pl.program_id(1)
    rows = C_HEADS * tq
    cols = PAGE * C_HEADS

    @pl.when(j == 0)
    def _():
        m_scr[...] = jnp.full_like(m_scr, NEG_BIG)
        l_scr[...] = jnp.zeros_like(l_scr)
        acc_scr[...] = jnp.zeros_like(acc_scr)
        carry_scr[...] = jnp.zeros_like(carry_scr)
        r_i = lax.broadcasted_iota(jnp.int32, (rows, cols), 0)
        c_i = lax.broadcasted_iota(jnp.int32, (rows, cols), 1)
        madd_scr[...] = jnp.where(c_i % C_HEADS == r_i // tq, 0.0, NEG_BIG) + gq_ref[...]

    q = q_ref[...]

    def scores(k_blk, row_bias):
        k2 = k_blk.reshape(cols, C_DH).astype(BF16)
        return _dot_nt(q, k2) * C_SCALE + madd_scr[...] + row_bias

    def update(s_list, v_list):
        m_old = m_scr[...]
        m_new = m_old
        for s in s_list:
            m_new = jnp.maximum(m_new, jnp.max(s, axis=-1, keepdims=True))
        alpha = jnp.exp(m_old - m_new)
        l_new = alpha * l_scr[...]
        pv = None
        for s, v in zip(s_list, v_list):
            p = jnp.exp(s - m_new)
            l_new = l_new + jnp.sum(p, axis=-1, keepdims=True)
            d = _dot(p.astype(BF16), v.reshape(cols, C_DH).astype(BF16))
            pv = d if pv is None else pv + d
        m_scr[...] = m_new
        l_scr[...] = l_new
        acc_scr[...] = alpha * acc_scr[...] + pv

    carry = carry_scr[...]
    s_list = []
    for u in range(pp):
        s_list.append(scores(k_refs[u][...], rin_refs[u][...] + carry))
        carry = carry + tot_refs[u][...]
    carry_scr[...] = carry
    update(s_list, [v_refs[u][...] for u in range(pp)])

    @pl.when(j == nsteps - 1)
    def _():
        r_i = lax.broadcasted_iota(jnp.int32, (rows, cols), 0)
        c_i = lax.broadcasted_iota(jnp.int32, (rows, cols), 1)
        causal = jnp.where(c_i // C_HEADS <= r_i % tq, 0.0, NEG_BIG)
        update([scores(kn_ref[...], -gk_ref[...]) + causal], [vn_ref[...]])
        o_ref[...] = acc_scr[...] / l_scr[...]


def _fox_sample(cq, ck, cv, fcum, cache_k, cache_v, rin, tot, page_table, l, bsz, t):
    n_pages = page_table.shape[1]
    pp = 4 if n_pages % 4 == 0 else 1
    nsteps = n_pages // pp
    rows = C_HEADS * t
    cols = PAGE * C_HEADS
    q2 = cq.reshape(bsz, t, C_HEADS, C_DH).transpose(0, 2, 1, 3).reshape(bsz, rows, C_DH)
    g = fcum[:, :C_HEADS].reshape(bsz, t, C_HEADS)
    gq = g.transpose(0, 2, 1).reshape(bsz, rows, 1)
    gk = jnp.pad(g.reshape(bsz, 1, t * C_HEADS), ((0, 0), (0, 0), (0, cols - t * C_HEADS)))
    tok_pad = ((0, 0), (0, PAGE - t), (0, 0), (0, 0))
    kn = jnp.pad(ck.reshape(bsz, t, C_HEADS, C_DH), tok_pad)
    vn = jnp.pad(cv.reshape(bsz, t, C_HEADS, C_DH), tok_pad)

    def page_idx(b, j, pt, u):
        return pt[b * n_pages + n_pages - 1 - (j * pp + u)]

    def kv_spec(u):
        return pl.BlockSpec((None, None, PAGE, C_HEADS, C_DH),
                            lambda b, j, pt: (l, page_idx(b, j, pt, u), 0, 0, 0))

    def vec_spec(u):
        return pl.BlockSpec((None, None, 1, cols), lambda b, j, pt: (l, page_idx(b, j, pt, u), 0, 0))

    batch3 = lambda b, j, pt: (b, 0, 0)
    batch4 = lambda b, j, pt: (b, 0, 0, 0)
    in_specs = [pl.BlockSpec((None, rows, C_DH), batch3),
                pl.BlockSpec((None, rows, 1), batch3),
                pl.BlockSpec((None, 1, cols), batch3),
                pl.BlockSpec((None, PAGE, C_HEADS, C_DH), batch4),
                pl.BlockSpec((None, PAGE, C_HEADS, C_DH), batch4)]
    in_specs += [kv_spec(u) for u in range(pp)]
    in_specs += [kv_spec(u) for u in range(pp)]
    in_specs += [vec_spec(u) for u in range(pp)]
    in_specs += [vec_spec(u) for u in range(pp)]
    grid_spec = pltpu.PrefetchScalarGridSpec(
        num_scalar_prefetch=1,
        grid=(bsz, nsteps),
        in_specs=in_specs,
        out_specs=pl.BlockSpec((None, rows, C_DH), batch3),
        scratch_shapes=[pltpu.VMEM((rows, 1), F32), pltpu.VMEM((rows, 1), F32),
                        pltpu.VMEM((rows, C_DH), F32), pltpu.VMEM((1, cols), F32),
                        pltpu.VMEM((rows, cols), F32)],
    )
    out = pl.pallas_call(
        functools.partial(_fox_sample_kernel, pp=pp, nsteps=nsteps, tq=t),
        grid_spec=grid_spec,
        out_shape=jax.ShapeDtypeStruct((bsz, rows, C_DH), F32),
        compiler_params=_cp("arbitrary", "arbitrary"),
        name="fox_sample",
    )(page_table.reshape(-1), q2, gq, gk, kn, vn,
      *([cache_k] * pp), *([cache_v] * pp), *([rin] * pp), *([tot] * pp))
    return out.reshape(bsz, C_HEADS, t, C_DH).transpose(0, 2, 1, 3).reshape(bsz * t, C_WIDTH)


def _merge_kernel(h_ref, oa_ref, ob_ref, oc_ref, ga_ref, gb_ref, gc_ref, wa_ref, wb_ref, wc_ref, o_ref,
                  wab, wbb, wcb):
    @pl.when(pl.program_id(1) == 0)
    def _():
        wab[...] = wa_ref[...].astype(BF16)
        wbb[...] = wb_ref[...].astype(BF16)
        wcb[...] = wc_ref[...].astype(BF16)

    h = h_ref[...]
    out = _sigmoid(_dot_nt(h, ga_ref[...])) * _dot(oa_ref[...].astype(BF16), wab[...])
    out = out + _sigmoid(_dot_nt(h, gb_ref[...])) * _dot(ob_ref[...].astype(BF16), wbb[...])
    out = out + _sigmoid(_dot_nt(h, gc_ref[...])) * _dot(oc_ref[...].astype(BF16), wcb[...])
    o_ref[...] = out.astype(o_ref.dtype)


def _merge(h, oa, ob, oc, wg_t, wpa, wpb, wpc, l):
    m, d = h.shape
    tm = _row_tile(m, ROW_TILE // 2)
    tn = _tile(d, 256, LANES)
    nb = d // tn
    act = lambda w: pl.BlockSpec((tm, w), lambda j, i: (i, 0))
    wgt = lambda w: pl.BlockSpec((None, w, tn), lambda j, i: (l, 0, j))
    gate = lambda g: pl.BlockSpec((None, tn, d), lambda j, i: (l, g * nb + j, 0))
    return pl.pallas_call(
        _merge_kernel,
        grid=(nb, m // tm),
        in_specs=[act(d), act(A_WIDTH), act(B_WIDTH), act(C_WIDTH), gate(0), gate(1), gate(2),
                  wgt(A_WIDTH), wgt(B_WIDTH), wgt(C_WIDTH)],
        out_specs=pl.BlockSpec((tm, tn), lambda j, i: (i, j)),
        out_shape=jax.ShapeDtypeStruct((m, d), BF16),
        scratch_shapes=[pltpu.VMEM((A_WIDTH, tn), BF16), pltpu.VMEM((B_WIDTH, tn), BF16),
                        pltpu.VMEM((C_WIDTH, tn), BF16)],
        compiler_params=_cp("arbitrary", "arbitrary"),
        name="gated_merge",
    )(h, oa, ob, oc, wg_t, wg_t, wg_t, wpa, wpb, wpc)


def _prep_w_in(w_in):
    depth, d, _ = w_in.shape
    w_t = jnp.swapaxes(w_in, 1, 2)
    w_c = w_t[:, _O_C:_O_CF].astype(BF16)
    w_g = w_t[:, _O_G:].astype(BF16)
    pad = jnp.zeros((depth, LANES - C_HEADS - 2 * B_HEADS, d), BF16)
    w_s = jnp.concatenate([w_t[:, _O_CF:_O_G].astype(BF16), w_t[:, _O_BI:_O_C].astype(BF16), pad], axis=1)
    return w_t, w_c, w_g, w_s


def _layer(x, l, w, groups, past, kv_stacks):
    (ffn1_norm, ffn1_w1, ffn1_w3, ffn1_w2, mix_norm, w_t, w_c, w_g, w_s, lb_all, hgrn_norm, gate_bias,
     mlstm_norm, w_proj_a, w_proj_b, w_proj_c, w_out, ffn2_norm, ffn2_w1, ffn2_w3, ffn2_w2) = w

    h = _rmsnorm(x, ffn1_norm[l], BF16)
    x = _down(_ffn_up(h, ffn1_w1, ffn1_w3, l), ffn1_w2, x, l, 0.5)

    h = _rmsnorm(x, mix_norm[l], BF16)
    za = _proj(h, w_t, l, 0, 4 * A_WIDTH, F32)
    zb = _proj(h, w_t, l, _O_B, ZB_W, F32)
    zs = _proj(h, w_s, l, 0, LANES, F32)
    cq = _proj(h, w_c, l, 0, C_WIDTH, BF16)
    depth = w_c.shape[0]
    k_stack, ckb = _proj_kv(h, w_c, l, C_WIDTH, C_WIDTH, kv_stacks[0], depth)
    v_stack, cvb = _proj_kv(h, w_c, l, 2 * C_WIDTH, C_WIDTH, kv_stacks[1], depth)

    branch_out = []
    states_out = []
    for gi, (bsz, t, row0, (s_a0, c0, n0, m0)) in enumerate(groups):
        rows = slice(row0, row0 + bsz * t)
        oa, s_a = _hgrn(za, lb_all[l], hgrn_norm[l], s_a0, bsz, t, row0)
        if gi == 0:
            gact, fcum, frow = _gates(zs, gate_bias[l], bsz, t, row0, True)
        else:
            gact, fcum = _gates(zs, gate_bias[l], bsz, t, row0, False)
        ob, c_b, n_b, m_b = _mlstm(zb, gact, mlstm_norm[l], c0, n0, m0, bsz, t, row0)
        if gi == 0:
            assert row0 == 0
            oc = _fox_prompt(cq, ckb, cvb, frow, fcum, bsz, t)
        else:
            oc = _fox_sample(cq[rows], k_stack[l, rows], v_stack[l, rows], fcum, *past, l, bsz, t)
        branch_out.append((oa.astype(BF16), ob.astype(BF16), oc.astype(BF16)))
        states_out.append((gact[:, :C_HEADS].reshape(bsz, t, C_HEADS), s_a, c_b,
                           n_b.reshape(bsz, B_HEADS, B_DK), m_b[:, :, 0, 0]))
    oa, ob, oc = (jnp.concatenate(parts, axis=0) for parts in zip(*branch_out))

    merged = _merge(h, oa, ob, oc, w_g, w_proj_a, w_proj_b, w_proj_c, l)
    x = _down(merged, w_out, x, l, 1.0)

    h = _rmsnorm(x, ffn2_norm[l], BF16)
    x = _down(_ffn_up(h, ffn2_w1, ffn2_w3, l), ffn2_w2, x, l, 0.5)
    return x, states_out, (k_stack, v_stack)


def kernel(x_prompt, x_sample, cache_k, cache_v, cache_logf, state_hgrn, state_mlstm_c, state_mlstm_n,
           state_mlstm_m, page_table, ffn1_norm, ffn1_w1, ffn1_w3, ffn1_w2, mix_norm, w_in, hgrn_lb,
           hgrn_norm, mlstm_b_i, mlstm_b_f, mlstm_norm, fox_b_f, w_proj_a, w_proj_b, w_proj_c, w_out,
           ffn2_norm, ffn2_w1, ffn2_w3, ffn2_w2, final_norm):
    depth = w_in.shape[0]
    bp, tp, d = x_prompt.shape
    bs, ts, _ = x_sample.shape
    n_pool = cache_k.shape[1]

    lb_all = _lower_bounds(hgrn_lb)
    w_t, w_c, w_g, w_s = _prep_w_in(w_in)
    gate_bias = jnp.concatenate(
        [fox_b_f, mlstm_b_i, mlstm_b_f, jnp.zeros((depth, LANES - C_HEADS - 2 * B_HEADS), F32)],
        axis=-1).reshape(depth, 1, LANES).astype(F32)
    w = (ffn1_norm, ffn1_w1, ffn1_w3, ffn1_w2.astype(BF16), mix_norm, w_t, w_c, w_g, w_s, lb_all, hgrn_norm,
         gate_bias, mlstm_norm, w_proj_a, w_proj_b, w_proj_c, w_out, ffn2_norm, ffn2_w1, ffn2_w3,
         ffn2_w2.astype(BF16))

    lft = jnp.swapaxes(cache_logf.astype(F32), -1, -2).reshape(depth * n_pool, C_HEADS, PAGE)
    flat = lambda a: jnp.swapaxes(a, -1, -2).reshape(depth, n_pool, 1, PAGE * C_HEADS)
    rin, tot = (flat(a) for a in _page_suffix(lft))

    zero_states = (jnp.zeros((bp, A_HEADS, A_DK, A_DV), F32), jnp.zeros((bp, B_HEADS, B_DK, B_DV), F32),
                   jnp.zeros((bp, B_HEADS, B_DK), F32), jnp.zeros((bp, B_HEADS), F32))

    mp = bp * tp
    x = jnp.concatenate([x_prompt.reshape(mp, d), x_sample.reshape(bs * ts, d)], axis=0)
    p_states = [[] for _ in range(5)]
    s_states = [[] for _ in range(5)]
    kv = (None, None)
    for l in range(depth):
        groups = ((bp, tp, 0, zero_states),
                  (bs, ts, mp, (state_hgrn[l], state_mlstm_c[l], state_mlstm_n[l], state_mlstm_m[l])))
        x, (st_p, st_s), kv = _layer(x, l, w, groups, (cache_k, cache_v, rin, tot, page_table), kv)
        for i in range(5):
            p_states[i].append(st_p[i])
            s_states[i].append(st_s[i])

    y = _rmsnorm(x, final_norm, F32)
    kv_p = [a[:, :mp].reshape(depth, bp, tp, C_HEADS, C_DH) for a in kv]
    kv_s = [a[:, mp:].reshape(depth, bs, ts, C_HEADS, C_DH) for a in kv]
    outs_p = [jnp.stack(s, axis=0) for s in p_states]
    outs_s = [jnp.stack(s, axis=0) for s in s_states]
    return (y[:mp].reshape(bp, tp, d), y[mp:].reshape(bs, ts, d), *kv_p, *outs_p, *kv_s, *outs_s)
```
